```python
import math
import jax, jax.numpy as jnp
from jax import lax
import numpy as np

D_MODEL = 1024
BATCH = 4
SEQ = 8192
DEPTH = 2
DEC_BATCH = 2
DEC_SEQ = 16384
PAST_LEN = 128

HEAD_DIM = 64
BLOCK = 128
GRID_W = 64
LN_EPS = 1e-5
RMS_EPS = 1e-6
ROPE_THETA = 500000.0
ROPE_DIMS = HEAD_DIM // 4
AXIAL_THETA = 10000.0
A_GROUPS = 8
A_WIDTH = A_GROUPS * HEAD_DIM
A_CHUNK = 128
B_HEADS = 4
B_QK_DIM = HEAD_DIM
B_V_DIM = 2 * HEAD_DIM
B_WIDTH = B_HEADS * B_V_DIM
B_LAYER_IDX = 0
B_LAMBDA_INIT = 0.8 - 0.6 * math.exp(-0.3 * B_LAYER_IDX)
C_Q_HEADS = 8
C_KV_HEADS = 2
C_GROUP = C_Q_HEADS // C_KV_HEADS
C_WIDTH = C_Q_HEADS * HEAD_DIM
D_PATTERNS = ((128, 1), (512, 4), (2048, 16))
D_N_PAT = 3
D_SLOTS = 4
D_N_KEYS = 129
D_WIDTH = D_SLOTS * HEAD_DIM
D_FF = 2816
CONV_W = 3
ALPHA = (2 * DEPTH) ** 0.25
BETA = (8 * DEPTH) ** -0.25
IN0 = 2 * A_WIDTH + B_HEADS * (2 * 2 * B_QK_DIM + B_V_DIM)
IN1 = C_WIDTH + 2 * C_KV_HEADS * HEAD_DIM + 3 * D_N_PAT * D_WIDTH
OUT0 = A_WIDTH + B_WIDTH
OUT1 = C_WIDTH + D_WIDTH

kernel_name = "hybrid_bidir_encoder_gmlp_diffattn_axialgqa_dilated"


def layer_norm(x, g, b):
    xf = x.astype(jnp.float32)
    mu = jnp.mean(xf, -1, keepdims=True)
    var = jnp.mean(jnp.square(xf - mu), -1, keepdims=True)
    return ((xf - mu) * lax.rsqrt(var + LN_EPS) * g + b).astype(x.dtype)


def rms_norm(x, g):
    xf = x.astype(jnp.float32)
    return (xf * lax.rsqrt(jnp.mean(xf * xf, -1, keepdims=True) + RMS_EPS) * g).astype(x.dtype)


def rope_cos_sin(pos, n_dims, theta):
    inv = jnp.power(theta, -jnp.arange(0, n_dims, 2, dtype=jnp.float32) / n_dims)
    ang = pos.astype(jnp.float32)[:, None] * inv[None, :]
    return jnp.cos(ang), jnp.sin(ang)


def apply_rope(x, cos, sin):
    half = cos.shape[-1]
    x1, x2, rest = x[..., :half], x[..., half:2 * half], x[..., 2 * half:]
    cos = cos.astype(x.dtype)
    sin = sin.astype(x.dtype)
    return jnp.concatenate([x1 * cos - x2 * sin, x2 * cos + x1 * sin, rest], -1)


def expand_pos(c, n_mid):
    return c.reshape(c.shape[0], *([1] * n_mid), c.shape[-1])


def to_blocks(x):
    b, s = x.shape[:2]
    return jnp.moveaxis(x.reshape(b, s // BLOCK, BLOCK, *x.shape[2:]), 1, 0)


def from_blocks(y):
    n, b = y.shape[:2]
    return jnp.moveaxis(y, 0, 1).reshape(b, n * y.shape[2], *y.shape[3:])


def gmlp_chunk_mixer(z, ln_g, ln_b, w_s, b_s):
    u, v = jnp.split(z, 2, -1)
    v = layer_norm(v, ln_g, ln_b)
    bsz, s, _ = v.shape
    v = v.reshape(bsz, s // A_CHUNK, A_CHUNK, A_GROUPS, HEAD_DIM)
    mixed = jnp.einsum('gpq,bnqgc->bnpgc', w_s, v) + b_s.T[:, :, None]
    return u * mixed.reshape(bsz, s, A_WIDTH)


def diff_attention(q, k, v, lam_q1, lam_k1, lam_q2, lam_k2, subln_g, cos_p, sin_p):
    bsz, s = q.shape[:2]
    c5, s5 = expand_pos(cos_p, 2), expand_pos(sin_p, 2)
    q = apply_rope(q, c5, s5)
    k = apply_rope(k, c5, s5)
    f32 = jnp.float32
    lam = (jnp.exp(jnp.sum(lam_q1.astype(f32) * lam_k1.astype(f32)))
           - jnp.exp(jnp.sum(lam_q2.astype(f32) * lam_k2.astype(f32))) + B_LAMBDA_INIT)
    scale = B_QK_DIM ** -0.5

    def block(qb):
        sc = jnp.einsum('bqhmd,bkhmd->bhmqk', qb, k).astype(f32) * scale
        p = jax.nn.softmax(sc, -1)
        w = (p[:, :, 0] - lam * p[:, :, 1]).astype(v.dtype)
        return jnp.einsum('bhqk,bkhe->bqhe', w, v)

    o = from_blocks(lax.map(block, to_blocks(q)))
    o = rms_norm(o, subln_g) * (1.0 - B_LAMBDA_INIT)
    return o.reshape(bsz, s, B_WIDTH)


def axial_gqa(q, k, v, qn_g, kn_g, cos_r, sin_r, cos_c, sin_c):
    bsz, s = q.shape[:2]
    half = HEAD_DIM // 2
    q = rms_norm(q, qn_g)
    k = rms_norm(k, kn_g)

    def axial(x, n_mid):
        return jnp.concatenate([
            apply_rope(x[..., :half], expand_pos(cos_r, n_mid), expand_pos(sin_r, n_mid)),
            apply_rope(x[..., half:], expand_pos(cos_c, n_mid), expand_pos(sin_c, n_mid))], -1)

    q = axial(q, 2)
    k = axial(k, 1)
    scale = HEAD_DIM ** -0.5

    def block(qb):
        sc = jnp.einsum('bqkgd,bskd->bkgqs', qb, k).astype(jnp.float32) * scale
        p = jax.nn.softmax(sc, -1).astype(v.dtype)
        return jnp.einsum('bkgqs,bskd->bqkgd', p, v)

    o = from_blocks(lax.map(block, to_blocks(q)))
    return o.reshape(bsz, s, C_WIDTH)


def dilated_attention(q, k, v, cos_p, sin_p):
    bsz, s = q.shape[:2]
    c5, s5 = expand_pos(cos_p, 2), expand_pos(sin_p, 2)
    q = apply_rope(q, c5, s5)
    k = apply_rope(k, c5, s5)
    offsets = jnp.asarray(np.stack([np.arange(-(w // 2), w // 2 + 1, r) for w, r in D_PATTERNS]).astype(np.int32))
    p_idx = jnp.arange(D_N_PAT)[:, None, None]
    scale = HEAD_DIM ** -0.5
    neg = jnp.float32(-1e30)

    def block(args):
        qb, i = args
        t = i * BLOCK + jnp.arange(BLOCK)
        idx = t[None, :, None] + offsets[:, None, :]
        valid = (idx >= 0) & (idx < s)
        idx = jnp.clip(idx, 0, s - 1)
        kg = k[:, idx, p_idx]
        vg = v[:, idx, p_idx]
        sc = jnp.einsum('bqphd,bpqjhd->bhqpj', qb, kg).astype(jnp.float32) * scale
        sc = jnp.where(jnp.transpose(valid, (1, 0, 2))[None, None], sc, neg)
        w = jax.nn.softmax(sc.reshape(bsz, D_SLOTS, BLOCK, D_N_PAT * D_N_KEYS), -1)
        w = w.reshape(sc.shape).astype(v.dtype)
        return jnp.einsum('bhqpj,bpqjhd->bqhd', w, vg)

    o = from_blocks(lax.map(block, (to_blocks(q), jnp.arange(s // BLOCK))))
    return o.reshape(bsz, s, D_WIDTH)


def mixer_ab(x, w_in, a_ln_g, a_ln_b, a_ws, a_bs, lq1, lk1, lq2, lk2, subln_g, w_out, cos_p, sin_p):
    bsz, s, _ = x.shape
    h = x @ w_in
    n_a = 2 * A_WIDTH
    n_qk = B_HEADS * 2 * B_QK_DIM
    z_a, q_b, k_b, v_b = jnp.split(h, [n_a, n_a + n_qk, n_a + 2 * n_qk], -1)
    ya = gmlp_chunk_mixer(jax.nn.gelu(z_a), a_ln_g, a_ln_b, a_ws, a_bs)
    yb = diff_attention(q_b.reshape(bsz, s, B_HEADS, 2, B_QK_DIM),
                        k_b.reshape(bsz, s, B_HEADS, 2, B_QK_DIM),
                        v_b.reshape(bsz, s, B_HEADS, B_V_DIM),
                        lq1, lk1, lq2, lk2, subln_g, cos_p, sin_p)
    return jnp.concatenate([ya, yb], -1) @ w_out


def mixer_cd(x, w_in, qn_g, kn_g, w_out, cos_p, sin_p, cos_r, sin_r, cos_c, sin_c):
    bsz, s, _ = x.shape
    h = x @ w_in
    nq = C_WIDTH
    nkv = C_KV_HEADS * HEAD_DIM
    q_c, k_c, v_c, qkv_d = jnp.split(h, [nq, nq + nkv, nq + 2 * nkv], -1)
    yc = axial_gqa(q_c.reshape(bsz, s, C_KV_HEADS, C_GROUP, HEAD_DIM),
                   k_c.reshape(bsz, s, C_KV_HEADS, HEAD_DIM),
                   v_c.reshape(bsz, s, C_KV_HEADS, HEAD_DIM),
                   qn_g, kn_g, cos_r, sin_r, cos_c, sin_c)
    qkv_d = qkv_d.reshape(bsz, s, 3, D_N_PAT, D_SLOTS, HEAD_DIM)
    yd = dilated_attention(qkv_d[:, :, 0], qkv_d[:, :, 1], qkv_d[:, :, 2], cos_p, sin_p)
    return jnp.concatenate([yc, yd], -1) @ w_out


def conv_ffn(x, w_up, conv_w, conv_b, w_down):
    h = x @ w_up
    h = lax.conv_general_dilated(h, conv_w[:, None, :], window_strides=(1,), padding=((1, 1),),
                                 dimension_numbers=('NWC', 'WIO', 'NWC'),
                                 feature_group_count=h.shape[-1]) + conv_b
    a, g = jnp.split(h, 2, -1)
    return (jax.nn.gelu(g) * a) @ w_down


def trunk(x, mix_params, norm_params, ffn_params):
    s = x.shape[1]
    rows = s // GRID_W
    pos = jnp.arange(s)
    row_id = jnp.repeat(jnp.arange(rows), GRID_W)
    col_id = jnp.tile(jnp.arange(GRID_W), rows)
    cos_p, sin_p = rope_cos_sin(pos, ROPE_DIMS, ROPE_THETA)
    cos_r, sin_r = rope_cos_sin(row_id, HEAD_DIM // 2, AXIAL_THETA)
    cos_c, sin_c = rope_cos_sin(col_id, HEAD_DIM // 2, AXIAL_THETA)
    for layer in range(DEPTH):
        if layer % 2 == 0:
            y = mixer_ab(x, *mix_params[layer], cos_p, sin_p)
        else:
            y = mixer_cd(x, *mix_params[layer], cos_p, sin_p, cos_r, sin_r, cos_c, sin_c)
        g1, b1, g2, b2 = norm_params[layer]
        x = layer_norm(ALPHA * x + y, g1, b1)
        x = layer_norm(ALPHA * x + conv_ffn(x, *ffn_params[layer]), g2, b2)
    return x


def setup_inputs(seed: int = 0) -> dict:
    key = jax.random.key(seed)
    ks = iter(jax.random.split(key, 64))

    def nrm(shape, scale):
        return jax.random.normal(next(ks), shape, jnp.float32) * scale

    def gain(n):
        return 1.0 + nrm((n,), 0.02)

    def bias(n):
        return nrm((n,), 0.02)

    d = D_MODEL
    return {
        "x_prompt": nrm((BATCH, SEQ, d), 1.0),
        "x_sample": nrm((DEC_BATCH, DEC_SEQ, d), 1.0),
        "w_in0": nrm((d, IN0), d ** -0.5),
        "a_ln_g": gain(A_WIDTH),
        "a_ln_b": bias(A_WIDTH),
        "a_ws": nrm((A_GROUPS, A_CHUNK, A_CHUNK), A_CHUNK ** -0.5),
        "a_bs": 1.0 + nrm((A_GROUPS, A_CHUNK), 0.02),
        "b_lam_q1": nrm((B_QK_DIM,), 0.1),
        "b_lam_k1": nrm((B_QK_DIM,), 0.1),
        "b_lam_q2": nrm((B_QK_DIM,), 0.1),
        "b_lam_k2": nrm((B_QK_DIM,), 0.1),
        "b_subln_g": gain(B_V_DIM),
        "w_out0": nrm((OUT0, d), BETA * OUT0 ** -0.5),
        "ln0a_g": gain(d),
        "ln0a_b": bias(d),
        "w_up0": nrm((d, 2 * D_FF), d ** -0.5),
        "conv_w0": nrm((CONV_W, 2 * D_FF), CONV_W ** -0.5),
        "conv_b0": bias(2 * D_FF),
        "w_down0": nrm((D_FF, d), BETA * D_FF ** -0.5),
        "ln0b_g": gain(d),
        "ln0b_b": bias(d),
        "w_in1": nrm((d, IN1), d ** -0.5),
        "c_qnorm_g": gain(HEAD_DIM),
        "c_knorm_g": gain(HEAD_DIM),
        "w_out1": nrm((OUT1, d), BETA * OUT1 ** -0.5),
        "ln1a_g": gain(d),
        "ln1a_b": bias(d),
        "w_up1": nrm((d, 2 * D_FF), d ** -0.5),
        "conv_w1": nrm((CONV_W, 2 * D_FF), CONV_W ** -0.5),
        "conv_b1": bias(2 * D_FF),
        "w_down1": nrm((D_FF, d), BETA * D_FF ** -0.5),
        "ln1b_g": gain(d),
        "ln1b_b": bias(d),
    }


def reference(x_prompt, x_sample,
              w_in0, a_ln_g, a_ln_b, a_ws, a_bs, b_lam_q1, b_lam_k1, b_lam_q2, b_lam_k2, b_subln_g,
              w_out0, ln0a_g, ln0a_b, w_up0, conv_w0, conv_b0, w_down0, ln0b_g, ln0b_b,
              w_in1, c_qnorm_g, c_knorm_g, w_out1, ln1a_g, ln1a_b, w_up1, conv_w1, conv_b1, w_down1,
              ln1b_g, ln1b_b):
    mix_params = [
        (w_in0, a_ln_g, a_ln_b, a_ws, a_bs, b_lam_q1, b_lam_k1, b_lam_q2, b_lam_k2, b_subln_g, w_out0),
        (w_in1, c_qnorm_g, c_knorm_g, w_out1),
    ]
    norm_params = [(ln0a_g, ln0a_b, ln0b_g, ln0b_b), (ln1a_g, ln1a_b, ln1b_g, ln1b_b)]
    ffn_params = [(w_up0, conv_w0, conv_b0, w_down0), (w_up1, conv_w1, conv_b1, w_down1)]
    y_prompt = trunk(x_prompt, mix_params, norm_params, ffn_params)
    y_sample = trunk(x_sample, mix_params, norm_params, ffn_params)
    return (y_prompt, y_sample)
```

```python
import functools
import math

import jax
import jax.numpy as jnp
from jax import lax
from jax.experimental import pallas as pl
from jax.experimental.pallas import tpu as pltpu

F32 = jnp.float32
BF16 = jnp.bfloat16

D_MODEL = 1024
DEPTH = 2
HEAD_DIM = 64
GRID_W = 64
LN_EPS = 1e-5
RMS_EPS = 1e-6
ROPE_THETA = 500000.0
ROPE_DIMS = HEAD_DIM // 4
AXIAL_THETA = 10000.0
A_GROUPS = 8
A_WIDTH = A_GROUPS * HEAD_DIM
A_CHUNK = 128
B_HEADS = 4
B_V_DIM = 2 * HEAD_DIM
B_LAMBDA_INIT = 0.8 - 0.6 * math.exp(-0.3 * 0)
C_Q_HEADS = 8
C_KV_HEADS = 2
D_PATTERNS = ((128, 1), (512, 4), (2048, 16))
D_SLOTS = 4
D_WIDTH = D_SLOTS * HEAD_DIM
D_HALF_WINDOW = 64
D_FF = 2816
FF_CHUNK = 256
ALPHA = (2 * DEPTH) ** 0.25
QK_SCALE = HEAD_DIM ** -0.5
NEG_BIG = -1e30

LANES = 128
ONES_ROWS = 16
VMEM_LIMIT = 56 * 1024 * 1024


def _params(*sem):
    return pltpu.CompilerParams(dimension_semantics=sem, vmem_limit_bytes=VMEM_LIMIT)


def _resident(shape):
    nd = len(shape)
    return pl.BlockSpec(shape, lambda *_: (0,) * nd, pipeline_mode=pl.Buffered(1))


def _layer_norm(x, g, b):
    mu = jnp.mean(x, -1, keepdims=True)
    xc = x - mu
    var = jnp.mean(xc * xc, -1, keepdims=True)
    return xc * lax.rsqrt(var + LN_EPS) * g + b


def _rope(x, c, s_lo, s_hi, shift):
    return (x * c + pltpu.roll(x, LANES - shift, 1) * s_lo + pltpu.roll(x, shift, 1) * s_hi)


def _dot(a, b):
    return jnp.dot(a, b, preferred_element_type=F32)


def _inproj0_kernel(x_ref, w_ref, lng_ref, lnb_ref, c_ref, s1_ref, s2_ref,
                    u_ref, vn_ref, q_ref, k_ref, vb_ref):
    xb = x_ref[...].astype(BF16)
    za = jax.nn.gelu(_dot(xb, w_ref[:, 0:2 * A_WIDTH]))
    u_ref[...] = za[:, :A_WIDTH]
    vn_ref[...] = _layer_norm(za[:, A_WIDTH:], lng_ref[...], lnb_ref[...]).astype(BF16)
    c, s1, s2 = c_ref[...], s1_ref[...], s2_ref[...]
    base = 2 * A_WIDTH
    hq = _dot(xb, w_ref[:, base:base + 512])
    hk = _dot(xb, w_ref[:, base + 512:base + 1024])
    for j in range(4):
        sl = slice(j * LANES, (j + 1) * LANES)
        q_ref[:, sl] = (_rope(hq[:, sl], c, s1, s2, ROPE_DIMS // 2) * QK_SCALE).astype(BF16)
        k_ref[:, sl] = _rope(hk[:, sl], c, s1, s2, ROPE_DIMS // 2).astype(BF16)
    vb_ref[...] = _dot(xb, w_ref[:, base + 1024:base + 1536]).astype(BF16)


def _inproj0(x, w, lng, lnb, tabs, tm):
    bsz, s, d = x.shape
    n_in = w.shape[1]
    tok = lambda width: pl.BlockSpec((None, tm, width), lambda b, i: (b, i, 0))
    tab = pl.BlockSpec((tm, LANES), lambda b, i: (i, 0))
    outs = [jax.ShapeDtypeStruct((bsz, s, 512), F32)] + [jax.ShapeDtypeStruct((bsz, s, 512), BF16)] * 4
    return pl.pallas_call(
        _inproj0_kernel,
        grid=(bsz, s // tm),
        in_specs=[tok(d), _resident((d, n_in)), _resident((1, 512)), _resident((1, 512)), tab, tab, tab],
        out_specs=[tok(512)] * 5,
        out_shape=outs,
        compiler_params=_params("parallel", "parallel"),
        name="inproj0",
    )(x, w, lng, lnb, *tabs)


def _group_rms(x, gain, ones_blk):
    x2 = x * x
    hi = x2.astype(BF16)
    lo = (x2 - hi.astype(F32)).astype(BF16)
    ss = _dot(hi, ones_blk) + _dot(lo, ones_blk)
    return x * lax.rsqrt(ss * (1.0 / HEAD_DIM) + RMS_EPS) * gain


def _inproj1_kernel(x_ref, w_ref, qg_ref, kg_ref, ones_ref, c_ref, s1_ref, s2_ref,
                    ca_ref, sa1_ref, sa2_ref,
                    qc_ref, kc_ref, vc_ref, qd_ref, kd_ref, vd_ref):
    xb = x_ref[...].astype(BF16)
    ones_blk = ones_ref[...]
    ca, sa1, sa2 = ca_ref[...], sa1_ref[...], sa2_ref[...]
    hq = _dot(xb, w_ref[:, 0:512])
    for j in range(4):
        sl = slice(j * LANES, (j + 1) * LANES)
        qn = _group_rms(hq[:, sl], qg_ref[...], ones_blk)
        qc_ref[:, sl] = (_rope(qn, ca, sa1, sa2, HEAD_DIM // 4) * QK_SCALE).astype(BF16)
    hkv = _dot(xb, w_ref[:, 512:768])
    kn = _group_rms(hkv[:, :LANES], kg_ref[...], ones_blk)
    kc_ref[...] = _rope(kn, ca, sa1, sa2, HEAD_DIM // 4).astype(BF16)
    vc_ref[...] = hkv[:, LANES:].astype(BF16)
    c, s1, s2 = c_ref[...], s1_ref[...], s2_ref[...]
    hqd = _dot(xb, w_ref[:, 768:1536])
    hkd = _dot(xb, w_ref[:, 1536:2304])
    for j in range(6):
        sl = slice(j * LANES, (j + 1) * LANES)
        qd_ref[:, sl] = (_rope(hqd[:, sl], c, s1, s2, ROPE_DIMS // 2) * QK_SCALE).astype(BF16)
        kd_ref[:, sl] = _rope(hkd[:, sl], c, s1, s2, ROPE_DIMS // 2).astype(BF16)
    vd_ref[...] = _dot(xb, w_ref[:, 2304:3072]).astype(BF16)


def _inproj1(x, w, qg, kg, ones_blk, tabs_p, tabs_a, tm):
    bsz, s, d = x.shape
    n_in = w.shape[1]
    tok = lambda width: pl.BlockSpec((None, tm, width), lambda b, i: (b, i, 0))
    tab = pl.BlockSpec((tm, LANES), lambda b, i: (i, 0))
    widths = (512, 128, 128, 768, 768, 768)
    return pl.pallas_call(
        _inproj1_kernel,
        grid=(bsz, s // tm),
        in_specs=[tok(d), _resident((d, n_in)), _resident((1, LANES)), _resident((1, LANES)),
                  _resident((LANES, LANES))] + [tab] * 6,
        out_specs=[tok(wd) for wd in widths],
        out_shape=[jax.ShapeDtypeStruct((bsz, s, wd), BF16) for wd in widths],
        compiler_params=_params("parallel", "parallel"),
        name="inproj1",
    )(x, w, qg, kg, ones_blk, *tabs_p, *tabs_a)


def _flash_kernel(*refs, mode, dv, tq):
    if mode == "diff":
        (q_ref, k_ref, vt_ref, lq1_ref, lk1_ref, lq2_ref, lk2_ref, g_ref,
         o_ref, qp_ref, m_ref, acc_ref) = refs
    else:
        q_ref, k_ref, vt_ref, o_ref, qp_ref, m_ref, acc_ref = refs
    ki = pl.program_id(3)

    @pl.when(ki == 0)
    def _():
        row = lax.broadcasted_iota(jnp.int32, (2 * HEAD_DIM, tq), 0)
        if mode == "diff":
            q = q_ref[...]
            zero = jnp.zeros_like(q)
            qp_ref[0] = jnp.where(row < HEAD_DIM, q, zero)
            qp_ref[1] = jnp.where(row >= HEAD_DIM, q, zero)
        else:
            kv_head = pl.program_id(1) // 2
            keep = (row >= HEAD_DIM) == (kv_head == 1)
            for m in range(2):
                q = q_ref[m]
                qq = jnp.concatenate([q, q], axis=0)
                qp_ref[m] = jnp.where(keep, qq, jnp.zeros_like(qq))
        m_ref[...] = jnp.full(m_ref.shape, NEG_BIG, F32)
        acc_ref[...] = jnp.zeros(acc_ref.shape, F32)

    k = k_ref[...]
    vt = vt_ref[...]
    for m in range(2):
        s = _dot(k, qp_ref[m])
        m_old = m_ref[m]
        m_new = jnp.maximum(m_old, jnp.max(s, axis=0, keepdims=True))
        alpha = jnp.exp(m_old - m_new)
        p = jnp.exp(s - m_new).astype(BF16)
        acc_ref[m] = alpha * acc_ref[m] + _dot(vt, p)
        m_ref[m] = m_new

    @pl.when(ki == pl.num_programs(3) - 1)
    def _():
        a0, a1 = acc_ref[0], acc_ref[1]
        o0 = a0[:dv] * (1.0 / a0[dv:dv + 1])
        o1 = a1[:dv] * (1.0 / a1[dv:dv + 1])
        if mode == "diff":
            lam = (jnp.exp(jnp.sum(lq1_ref[...] * lk1_ref[...], keepdims=True))
                   - jnp.exp(jnp.sum(lq2_ref[...] * lk2_ref[...], keepdims=True)) + B_LAMBDA_INIT)
            o = o0 - lam * o1
            ms = jnp.mean(o * o, axis=0, keepdims=True)
            o = o * lax.rsqrt(ms + RMS_EPS) * g_ref[...] * (1.0 - B_LAMBDA_INIT)
        else:
            o = jnp.concatenate([o0, o1], axis=0)
        o_ref[...] = o.T.astype(o_ref.dtype)


def _flash(mode, qt, k, vt, extra, tq, tk):
    bsz, s = k.shape[0], k.shape[1]
    dv = vt.shape[2] - ONES_ROWS
    if mode == "diff":
        q_spec = pl.BlockSpec((None, None, 2 * HEAD_DIM, tq), lambda b, h, qi, ki: (b, h, 0, qi))
        k_spec = pl.BlockSpec((None, tk, LANES), lambda b, h, qi, ki: (b, ki, h))
        vt_spec = pl.BlockSpec((None, None, dv + ONES_ROWS, tk), lambda b, h, qi, ki: (b, h, 0, ki))
        extra_specs = [_resident((1, HEAD_DIM))] * 4 + [_resident((B_V_DIM, 1))]
    else:
        q_spec = pl.BlockSpec((None, None, 2, HEAD_DIM, tq), lambda b, h, qi, ki: (b, h, 0, 0, qi))
        k_spec = pl.BlockSpec((None, tk, LANES), lambda b, h, qi, ki: (b, ki, 0))
        vt_spec = pl.BlockSpec((None, None, dv + ONES_ROWS, tk), lambda b, h, qi, ki: (b, h // 2, 0, ki))
        extra_specs = []
    return pl.pallas_call(
        functools.partial(_flash_kernel, mode=mode, dv=dv, tq=tq),
        grid=(bsz, 4, s // tq, s // tk),
        in_specs=[q_spec, k_spec, vt_spec] + extra_specs,
        out_specs=pl.BlockSpec((None, tq, LANES), lambda b, h, qi, ki: (b, qi, h)),
        out_shape=jax.ShapeDtypeStruct((bsz, s, 512), BF16),
        scratch_shapes=[pltpu.VMEM((2, 2 * HEAD_DIM, tq), BF16),
                        pltpu.VMEM((2, 1, tq), F32),
                        pltpu.VMEM((2, dv + ONES_ROWS, tq), F32)],
        compiler_params=_params("parallel", "parallel", "parallel", "arbitrary"),
        name="flash_" + mode,
    )(qt, k, vt, *extra)


def _heads_transposed(x, n_heads, with_ones):
    bsz, s, width = x.shape
    xt = x.reshape(bsz, s, n_heads, width // n_heads).transpose(0, 2, 3, 1)
    if with_ones:
        xt = jnp.concatenate([xt, jnp.ones((bsz, n_heads, ONES_ROWS, s), x.dtype)], axis=2)
    return xt


def _dilated_kernel(q_ref, kp_ref, km_ref, kn_ref, vp_ref, vm_ref, vn_ref,
                    o_ref, m_ref, l_ref, *, tq, sub_len):
    i0 = pl.program_id(2) * tq
    q = q_ref[...]
    kw = jnp.concatenate([kp_ref[...], km_ref[...], kn_ref[...]], axis=0)
    vw = jnp.concatenate([vp_ref[...], vm_ref[...], vn_ref[...]], axis=0)
    nk = tq + 2 * LANES
    t = lax.broadcasted_iota(jnp.int32, (tq, nk), 0)
    w = lax.broadcasted_iota(jnp.int32, (tq, nk), 1)
    off = w - LANES - t
    j = i0 - LANES + w
    valid = (off >= -D_HALF_WINDOW) & (off <= D_HALF_WINDOW) & (j >= 0) & (j < sub_len)
    lane = lax.broadcasted_iota(jnp.int32, (1, D_WIDTH), 1)
    o_acc = jnp.zeros((tq, D_WIDTH), F32)
    m_acc = jnp.zeros((tq, D_WIDTH), F32)
    l_acc = jnp.zeros((tq, D_WIDTH), F32)
    for h in range(D_SLOTS):
        in_head = (lane >= h * HEAD_DIM) & (lane < (h + 1) * HEAD_DIM)
        qh = jnp.where(in_head, q, jnp.zeros_like(q))
        s = lax.dot_general(qh, kw, (((1,), (1,)), ((), ())), preferred_element_type=F32)
        s = jnp.where(valid, s, NEG_BIG)
        mh = jnp.max(s, axis=1, keepdims=True)
        p = jnp.exp(s - mh)
        lh = jnp.sum(p, axis=1, keepdims=True)
        oh = _dot(p.astype(BF16), vw)
        sel = in_head.astype(F32)
        o_acc = o_acc + oh * sel
        m_acc = m_acc + mh * sel
        l_acc = l_acc + lh * sel
    o_ref[...] = o_acc
    m_ref[...] = m_acc
    l_ref[...] = l_acc


def _dilated(qd, kd, vd, pat, dilation):
    bsz, s, width = qd.shape
    sub_len = s // dilation
    tq = min(256, sub_len)
    n_col = width // D_WIDTH
    view = lambda a: a.reshape(bsz, sub_len, dilation * width)
    blk128 = sub_len // LANES
    per_tile = tq // LANES
    main = pl.BlockSpec((None, tq, D_WIDTH), lambda b, c, i: (b, i, c * n_col + pat))
    prev = pl.BlockSpec((None, LANES, D_WIDTH),
                        lambda b, c, i: (b, jnp.maximum(i * per_tile - 1, 0), c * n_col + pat))
    nxt = pl.BlockSpec((None, LANES, D_WIDTH),
                       lambda b, c, i: (b, jnp.minimum((i + 1) * per_tile, blk128 - 1), c * n_col + pat))
    out_spec = pl.BlockSpec((None, tq, D_WIDTH), lambda b, c, i: (b, i, c))
    out = jax.ShapeDtypeStruct((bsz, sub_len, dilation * D_WIDTH), F32)
    o, m, l = pl.pallas_call(
        functools.partial(_dilated_kernel, tq=tq, sub_len=sub_len),
        grid=(bsz, dilation, sub_len // tq),
        in_specs=[main, prev, main, nxt, prev, main, nxt],
        out_specs=[out_spec] * 3,
        out_shape=[out] * 3,
        compiler_params=_params("parallel", "parallel", "parallel"),
        name="dilated_p%d" % pat,
    )(view(qd), view(kd), view(kd), view(kd), view(vd), view(vd), view(vd))
    return tuple(a.reshape(bsz, s, D_WIDTH) for a in (o, m, l))


def _mix0_kernel(x_ref, u_ref, vn_ref, yb_ref, ws_ref, bs_ref, w_ref, g_ref, b_ref, o_ref, *, tm):
    lane = lax.broadcasted_iota(jnp.int32, (A_CHUNK, LANES), 1)
    left = lane < HEAD_DIM
    rows = []
    for n in range(tm // A_CHUNK):
        r = slice(n * A_CHUNK, (n + 1) * A_CHUNK)
        cols = []
        for j in range(A_WIDTH // LANES):
            v = vn_ref[r, j * LANES:(j + 1) * LANES]
            zero = jnp.zeros_like(v)
            stacked = jnp.concatenate([jnp.where(left, v, zero), jnp.where(left, zero, v)], axis=0)
            cols.append(_dot(ws_ref[j], stacked))
        rows.append(jnp.concatenate(cols, axis=1) + bs_ref[...])
    mixed = jnp.concatenate(rows, axis=0)
    ya = (u_ref[...] * mixed).astype(BF16)
    y = _dot(ya, w_ref[0:A_WIDTH, :]) + _dot(yb_ref[...], w_ref[A_WIDTH:, :])
    o_ref[...] = _layer_norm(ALPHA * x_ref[...] + y, g_ref[...], b_ref[...])


def _mix0(x, u, vn, yb, ws_pair, bs_x, w_out, g, b, tm):
    bsz, s, d = x.shape
    tok = lambda width: pl.BlockSpec((None, tm, width), lambda bb, i: (bb, i, 0))
    return pl.pallas_call(
        functools.partial(_mix0_kernel, tm=tm),
        grid=(bsz, s // tm),
        in_specs=[tok(d), tok(512), tok(512), tok(512), _resident(ws_pair.shape), _resident(bs_x.shape),
                  _resident(w_out.shape), _resident((1, d)), _resident((1, d))],
        out_specs=tok(d),
        out_shape=jax.ShapeDtypeStruct((bsz, s, d), F32),
        compiler_params=_params("parallel", "parallel"),
        name="mix0",
    )(x, u, vn, yb, ws_pair, bs_x, w_out, g, b)


def _mix1_kernel(x_ref, yc_ref, *refs):
    pat_refs, (w_ref, g_ref, b_ref, o_ref) = refs[:9], refs[9:]
    os_, ms_, ls_ = pat_refs[0::3], pat_refs[1::3], pat_refs[2::3]
    m_all = jnp.maximum(jnp.maximum(ms_[0][...], ms_[1][...]), ms_[2][...])
    num = jnp.zeros(m_all.shape, F32)
    den = jnp.zeros(m_all.shape, F32)
    for p in range(3):
        wgt = jnp.exp(ms_[p][...] - m_all)
        num = num + wgt * os_[p][...]
        den = den + wgt * ls_[p][...]
    yd = (num * (1.0 / den)).astype(BF16)
    y = _dot(yc_ref[...], w_ref[0:512, :]) + _dot(yd, w_ref[512:, :])
    o_ref[...] = _layer_norm(ALPHA * x_ref[...] + y, g_ref[...], b_ref[...])


def _mix1(x, yc, pats, w_out, g, b, tm):
    bsz, s, d = x.shape
    tok = lambda width: pl.BlockSpec((None, tm, width), lambda bb, i: (bb, i, 0))
    flat = [a for oml in pats for a in oml]
    return pl.pallas_call(
        _mix1_kernel,
        grid=(bsz, s // tm),
        in_specs=[tok(d), tok(512)] + [tok(D_WIDTH)] * 9
                 + [_resident(w_out.shape), _resident((1, d)), _resident((1, d))],
        out_specs=tok(d),
        out_shape=jax.ShapeDtypeStruct((bsz, s, d), F32),
        compiler_params=_params("parallel", "parallel"),
        name="mix1",
    )(x, yc, *flat, w_out, g, b)


def _ffn_kernel(xp_ref, xm_ref, xn_ref, wa_ref, wg_ref, ca_ref, cg_ref, wd_ref, g_ref, b_ref,
                o_ref, xe_ref, acc_ref, *, tm, tiles_per_seq):
    i = pl.program_id(0) % tiles_per_seq
    halo = 8
    prev = jnp.where(i > 0, xp_ref[...], 0.0)
    nxt = jnp.where(i < tiles_per_seq - 1, xn_ref[...], 0.0)
    xe_ref[...] = jnp.concatenate([prev, xm_ref[...], nxt], axis=0).astype(BF16)
    acc_ref[...] = jnp.zeros(acc_ref.shape, F32)
    ext = tm + 2 * halo

    def conv(h, cw):
        mid = slice(halo, halo + tm)
        return (cw[0:1] * pltpu.roll(h, 1, 0)[mid] + cw[1:2] * h[mid]
                + cw[2:3] * pltpu.roll(h, ext - 1, 0)[mid] + cw[3:4])

    def body(c, carry):
        xe = xe_ref[...]
        a = conv(_dot(xe, wa_ref[c]), ca_ref[c])
        gt = conv(_dot(xe, wg_ref[c]), cg_ref[c])
        act = (jax.nn.gelu(gt) * a).astype(BF16)
        acc_ref[...] += _dot(act, wd_ref[c])
        return carry

    lax.fori_loop(0, wa_ref.shape[0], body, 0)
    o_ref[...] = _layer_norm(ALPHA * xm_ref[...] + acc_ref[...], g_ref[...], b_ref[...])


def _ffn(x, wa, wg, ca, cg, wd, g, b, tm):
    bsz, s, d = x.shape
    n_tok = bsz * s
    xf = x.reshape(n_tok, d)
    tiles_per_seq = s // tm
    r8 = tm // 8
    n8 = n_tok // 8
    out = pl.pallas_call(
        functools.partial(_ffn_kernel, tm=tm, tiles_per_seq=tiles_per_seq),
        grid=(n_tok // tm,),
        in_specs=[pl.BlockSpec((8, d), lambda i: (jnp.maximum(i * r8 - 1, 0), 0)),
                  pl.BlockSpec((tm, d), lambda i: (i, 0)),
                  pl.BlockSpec((8, d), lambda i: (jnp.minimum((i + 1) * r8, n8 - 1), 0)),
                  _resident(wa.shape), _resident(wg.shape), _resident(ca.shape), _resident(cg.shape),
                  _resident(wd.shape), _resident((1, d)), _resident((1, d))],
        out_specs=pl.BlockSpec((tm, d), lambda i: (i, 0)),
        out_shape=jax.ShapeDtypeStruct((n_tok, d), F32),
        scratch_shapes=[pltpu.VMEM((tm + 16, d), BF16), pltpu.VMEM((tm, d), F32)],
        compiler_params=_params("parallel"),
        name="conv_ffn",
    )(xf, xf, xf, wa, wg, ca, cg, wd, g, b)
    return out.reshape(bsz, s, d)


def _rope_tables(pos, n_dims, theta):
    inv = jnp.power(theta, -jnp.arange(0, n_dims, 2, dtype=F32) / n_dims)
    ang = pos.astype(F32)[:, None] * inv[None, :]
    return jnp.cos(ang), jnp.sin(ang)


def _lane_tables(parts, s):
    cs, lo, hi = [], [], []
    zero = None
    for cos, sin in parts:
        zero = jnp.zeros_like(sin)
        cs += [cos, cos]
        lo += [-sin, zero]
        hi += [zero, sin]
    used = sum(a.shape[1] for a in cs)
    pad1 = jnp.ones((s, HEAD_DIM - used), F32)
    pad0 = jnp.zeros((s, HEAD_DIM - used), F32)
    build = lambda xs, pad: jnp.tile(jnp.concatenate(xs + [pad], axis=1), (1, LANES // HEAD_DIM))
    return build(cs, pad1), build(lo, pad0), build(hi, pad0)


def _ffn_weights(w_up, conv_w, conv_b, w_down):
    d = w_up.shape[0]
    n_c = D_FF // FF_CHUNK
    chunks = lambda w: w.reshape(d, n_c, FF_CHUNK).transpose(1, 0, 2).astype(BF16)
    wa, wg = chunks(w_up[:, :D_FF]), chunks(w_up[:, D_FF:])

    def conv_pack(cw, cb):
        rows = jnp.concatenate([cw, cb[None, :], jnp.zeros((4, D_FF), F32)], axis=0)
        return rows.reshape(8, n_c, FF_CHUNK).transpose(1, 0, 2)

    ca = conv_pack(conv_w[:, :D_FF], conv_b[:D_FF])
    cg = conv_pack(conv_w[:, D_FF:], conv_b[D_FF:])
    wd = w_down.reshape(n_c, FF_CHUNK, d).astype(BF16)
    return wa, wg, ca, cg, wd


def _trunk(x, p, tm=512, tq=512, tk=512):
    bsz, s, d = x.shape
    tm, tq, tk = min(tm, s), min(tq, s), min(tk, s)
    pos = jnp.arange(s)
    tabs_p = _lane_tables([_rope_tables(pos, ROPE_DIMS, ROPE_THETA)], s)
    tabs_a = _lane_tables([_rope_tables(pos // GRID_W, HEAD_DIM // 2, AXIAL_THETA),
                           _rope_tables(pos % GRID_W, HEAD_DIM // 2, AXIAL_THETA)], s)
    row = lambda v: v.reshape(1, -1)

    u, vn, q0, k0, v0 = _inproj0(x, p["w_in0"], row(p["a_ln_g"]), row(p["a_ln_b"]), tabs_p, tm)
    yb = _flash("diff", _heads_transposed(q0, B_HEADS, False), k0, _heads_transposed(v0, B_HEADS, True),
                [row(p["b_lam_q1"]), row(p["b_lam_k1"]), row(p["b_lam_q2"]), row(p["b_lam_k2"]),
                 p["b_subln_g"].reshape(-1, 1)], tq, tk)
    x = _mix0(x, u, vn, yb, p["ws_pair"], p["bs_x"], p["w_out0"], row(p["ln0a_g"]), row(p["ln0a_b"]), tm)
    x = _ffn(x, *p["ffn0"], row(p["ln0b_g"]), row(p["ln0b_b"]), tm)

    qc, kc, vc, qd, kd, vd = _inproj1(x, p["w_in1"], p["qg"], p["kg"], p["ones_blk"], tabs_p, tabs_a, tm)
    yc = _flash("gqa", _heads_transposed(qc, C_Q_HEADS, False).reshape(bsz, 4, 2, HEAD_DIM, s), kc,
                _heads_transposed(vc, C_KV_HEADS, True), [], tq, tk)
    pats = [_dilated(qd, kd, vd, pi, dil) for pi, (_, dil) in enumerate(D_PATTERNS)]
    x = _mix1(x, yc, pats, p["w_out1"], row(p["ln1a_g"]), row(p["ln1a_b"]), tm)
    x = _ffn(x, *p["ffn1"], row(p["ln1b_g"]), row(p["ln1b_b"]), tm)
    return x


def _prepare(w_in0, a_ln_g, a_ln_b, a_ws, a_bs, b_lam_q1, b_lam_k1, b_lam_q2, b_lam_k2, b_subln_g,
             w_out0, ln0a_g, ln0a_b, w_up0, conv_w0, conv_b0, w_down0, ln0b_g, ln0b_b,
             w_in1, c_qnorm_g, c_knorm_g, w_out1, ln1a_g, ln1a_b, w_up1, conv_w1, conv_b1, w_down1,
             ln1b_g, ln1b_b):
    grp = jnp.arange(LANES) // HEAD_DIM
    return dict(
        w_in0=w_in0.astype(BF16), a_ln_g=a_ln_g, a_ln_b=a_ln_b,
        ws_pair=a_ws.reshape(A_GROUPS // 2, 2, A_CHUNK, A_CHUNK).transpose(0, 2, 1, 3)
        .reshape(A_GROUPS // 2, A_CHUNK, 2 * A_CHUNK).astype(BF16),
        bs_x=jnp.repeat(a_bs.T, HEAD_DIM, axis=1),
        b_lam_q1=b_lam_q1, b_lam_k1=b_lam_k1, b_lam_q2=b_lam_q2, b_lam_k2=b_lam_k2, b_subln_g=b_subln_g,
        w_out0=w_out0.astype(BF16), ln0a_g=ln0a_g, ln0a_b=ln0a_b,
        ffn0=_ffn_weights(w_up0, conv_w0, conv_b0, w_down0), ln0b_g=ln0b_g, ln0b_b=ln0b_b,
        w_in1=w_in1.astype(BF16),
        qg=jnp.tile(c_qnorm_g, LANES // HEAD_DIM).reshape(1, LANES),
        kg=jnp.tile(c_knorm_g, LANES // HEAD_DIM).reshape(1, LANES),
        ones_blk=(grp[:, None] == grp[None, :]).astype(BF16),
        w_out1=w_out1.astype(BF16), ln1a_g=ln1a_g, ln1a_b=ln1a_b,
        ffn1=_ffn_weights(w_up1, conv_w1, conv_b1, w_down1), ln1b_g=ln1b_g, ln1b_b=ln1b_b,
    )


def kernel(x_prompt, x_sample, w_in0, a_ln_g, a_ln_b, a_ws, a_bs, b_lam_q1, b_lam_k1, b_lam_q2, b_lam_k2, b_subln_g, w_out0, ln0a_g, ln0a_b, w_up0, conv_w0, conv_b0, w_down0, ln0b_g, ln0b_b, w_in1, c_qnorm_g, c_knorm_g, w_out1, ln1a_g, ln1a_b, w_up1, conv_w1, conv_b1, w_down1, ln1b_g, ln1b_b):
    p = _prepare(w_in0, a_ln_g, a_ln_b, a_ws, a_bs, b_lam_q1, b_lam_k1, b_lam_q2, b_lam_k2, b_subln_g,
                 w_out0, ln0a_g, ln0a_b, w_up0, conv_w0, conv_b0, w_down0, ln0b_g, ln0b_b,
                 w_in1, c_qnorm_g, c_knorm_g, w_out1, ln1a_g, ln1a_b, w_up1, conv_w1, conv_b1, w_down1,
                 ln1b_g, ln1b_b)
    return (_trunk(x_prompt, p), _trunk(x_sample, p))
```

```python
import functools
import math

import jax
import jax.numpy as jnp
from jax import lax
from jax.experimental import pallas as pl
from jax.experimental.pallas import tpu as pltpu

F32 = jnp.float32
BF16 = jnp.bfloat16

D_MODEL = 1024
DEPTH = 2
HEAD_DIM = 64
GRID_W = 64
LN_EPS = 1e-5
RMS_EPS = 1e-6
ROPE_THETA = 500000.0
ROPE_DIMS = HEAD_DIM // 4
AXIAL_THETA = 10000.0
A_GROUPS = 8
A_WIDTH = A_GROUPS * HEAD_DIM
A_CHUNK = 128
B_HEADS = 4
B_V_DIM = 2 * HEAD_DIM
B_LAMBDA_INIT = 0.8 - 0.6 * math.exp(-0.3 * 0)
C_Q_HEADS = 8
C_KV_HEADS = 2
D_PATTERNS = ((128, 1), (512, 4), (2048, 16))
D_SLOTS = 4
D_WIDTH = D_SLOTS * HEAD_DIM
D_HALF_WINDOW = 64
D_FF = 2816
FF_CHUNK = 256
ALPHA = (2 * DEPTH) ** 0.25
QK_SCALE = HEAD_DIM ** -0.5 * math.log2(math.e)
NEG_BIG = -1e30
BOUND_SLACK = 1.0 + 2.0 ** -6
SAFE_DENOM = 2.0 ** -64
KV_CHUNK = 1024

LANES = 128
ONES_ROWS = 16
VMEM_LIMIT = 56 * 1024 * 1024


def _params(*sem):
    return pltpu.CompilerParams(dimension_semantics=sem, vmem_limit_bytes=VMEM_LIMIT)


def _resident(shape):
    nd = len(shape)
    return pl.BlockSpec(shape, lambda *_: (0,) * nd, pipeline_mode=pl.Buffered(1))


def _layer_norm(x, g, b):
    mu = jnp.mean(x, -1, keepdims=True)
    xc = x - mu
    var = jnp.mean(xc * xc, -1, keepdims=True)
    return xc * lax.rsqrt(var + LN_EPS) * g + b


def _rope(x, c, s_lo, s_hi, shift):
    return (x * c + pltpu.roll(x, LANES - shift, 1) * s_lo + pltpu.roll(x, shift, 1) * s_hi)


def _dot(a, b):
    return jnp.dot(a, b, preferred_element_type=F32)


def _inproj0_kernel(x_ref, w_ref, lng_ref, lnb_ref, c_ref, s1_ref, s2_ref,
                    u_ref, vn_ref, q_ref, k_ref, vb_ref):
    xb = x_ref[...].astype(BF16)
    za = jax.nn.gelu(_dot(xb, w_ref[:, 0:2 * A_WIDTH]))
    u_ref[...] = za[:, :A_WIDTH]
    vn_ref[...] = _layer_norm(za[:, A_WIDTH:], lng_ref[...], lnb_ref[...]).astype(BF16)
    c, s1, s2 = c_ref[...], s1_ref[...], s2_ref[...]
    base = 2 * A_WIDTH
    hq = _dot(xb, w_ref[:, base:base + 512])
    hk = _dot(xb, w_ref[:, base + 512:base + 1024])
    for j in range(4):
        sl = slice(j * LANES, (j + 1) * LANES)
        q_ref[:, sl] = (_rope(hq[:, sl], c, s1, s2, ROPE_DIMS // 2) * QK_SCALE).astype(BF16)
        k_ref[:, sl] = _rope(hk[:, sl], c, s1, s2, ROPE_DIMS // 2).astype(BF16)
    vb_ref[...] = _dot(xb, w_ref[:, base + 1024:base + 1536]).astype(BF16)


def _inproj0(x, w, lng, lnb, tabs, tm):
    bsz, s, d = x.shape
    n_in = w.shape[1]
    tok = lambda width: pl.BlockSpec((None, tm, width), lambda b, i: (b, i, 0))
    tab = pl.BlockSpec((tm, LANES), lambda b, i: (i, 0))
    outs = [jax.ShapeDtypeStruct((bsz, s, 512), F32)] + [jax.ShapeDtypeStruct((bsz, s, 512), BF16)] * 4
    return pl.pallas_call(
        _inproj0_kernel,
        grid=(bsz, s // tm),
        in_specs=[tok(d), _resident((d, n_in)), _resident((1, 512)), _resident((1, 512)), tab, tab, tab],
        out_specs=[tok(512)] * 5,
        out_shape=outs,
        compiler_params=_params("parallel", "parallel"),
        name="inproj0",
    )(x, w, lng, lnb, *tabs)


def _group_rms(x, gain, ones_blk):
    x2 = x * x
    hi = x2.astype(BF16)
    lo = (x2 - hi.astype(F32)).astype(BF16)
    ss = _dot(hi, ones_blk) + _dot(lo, ones_blk)
    return x * lax.rsqrt(ss * (1.0 / HEAD_DIM) + RMS_EPS) * gain


def _inproj1_kernel(x_ref, w_ref, qg_ref, kg_ref, ones_ref, c_ref, s1_ref, s2_ref,
                    ca_ref, sa1_ref, sa2_ref,
                    qc_ref, kc_ref, vc_ref, qd_ref, kd_ref, vd_ref):
    xb = x_ref[...].astype(BF16)
    ones_blk = ones_ref[...]
    ca, sa1, sa2 = ca_ref[...], sa1_ref[...], sa2_ref[...]
    hq = _dot(xb, w_ref[:, 0:512])
    for j in range(4):
        sl = slice(j * LANES, (j + 1) * LANES)
        qn = _group_rms(hq[:, sl], qg_ref[...], ones_blk)
        qc_ref[:, sl] = (_rope(qn, ca, sa1, sa2, HEAD_DIM // 4) * QK_SCALE).astype(BF16)
    hkv = _dot(xb, w_ref[:, 512:768])
    kn = _group_rms(hkv[:, :LANES], kg_ref[...], ones_blk)
    kc_ref[...] = _rope(kn, ca, sa1, sa2, HEAD_DIM // 4).astype(BF16)
    vc_ref[...] = hkv[:, LANES:].astype(BF16)
    c, s1, s2 = c_ref[...], s1_ref[...], s2_ref[...]
    hqd = _dot(xb, w_ref[:, 768:1536])
    hkd = _dot(xb, w_ref[:, 1536:2304])
    for j in range(6):
        sl = slice(j * LANES, (j + 1) * LANES)
        qd_ref[:, sl] = (_rope(hqd[:, sl], c, s1, s2, ROPE_DIMS // 2) * QK_SCALE).astype(BF16)
        kd_ref[:, sl] = _rope(hkd[:, sl], c, s1, s2, ROPE_DIMS // 2).astype(BF16)
    vd_ref[...] = _dot(xb, w_ref[:, 2304:3072]).astype(BF16)


def _inproj1(x, w, qg, kg, ones_blk, tabs_p, tabs_a, tm):
    bsz, s, d = x.shape
    n_in = w.shape[1]
    tok = lambda width: pl.BlockSpec((None, tm, width), lambda b, i: (b, i, 0))
    tab = pl.BlockSpec((tm, LANES), lambda b, i: (i, 0))
    widths = (512, 128, 128, 768, 768, 768)
    return pl.pallas_call(
        _inproj1_kernel,
        grid=(bsz, s // tm),
        in_specs=[tok(d), _resident((d, n_in)), _resident((1, LANES)), _resident((1, LANES)),
                  _resident((LANES, LANES))] + [tab] * 6,
        out_specs=[tok(wd) for wd in widths],
        out_shape=[jax.ShapeDtypeStruct((bsz, s, wd), BF16) for wd in widths],
        compiler_params=_params("parallel", "parallel"),
        name="inproj1",
    )(x, w, qg, kg, ones_blk, *tabs_p, *tabs_a)


def _flash_kernel(*refs, mode, dv, tq, ck):
    if mode == "diff":
        (q_ref, k_ref, vt_ref, lq1_ref, lk1_ref, lq2_ref, lk2_ref, g_ref,
         o_ref, qp_ref, mb_ref, acc_ref, kmax_ref) = refs
    else:
        q_ref, k_ref, vt_ref, o_ref, qp_ref, mb_ref, acc_ref, kmax_ref = refs
    n_chunks = vt_ref.shape[0]
    kv_head = pl.program_id(1) // 2
    lane_grp = lax.broadcasted_iota(jnp.int32, (1, LANES), 1) // HEAD_DIM

    def k_chunk(c):
        return k_ref[pl.ds(pl.multiple_of(c * ck, ck), ck), :]

    @pl.when(pl.program_id(2) == 0)
    def _():
        r = lax.broadcasted_iota(jnp.int32, (LANES, LANES), 0) // HEAD_DIM
        c_ = lax.broadcasted_iota(jnp.int32, (LANES, LANES), 1) // HEAD_DIM
        ones_blk = (r == c_).astype(BF16)

        def body(c, mx):
            kf = k_chunk(c).astype(F32)
            return jnp.maximum(mx, jnp.max(_dot((kf * kf).astype(BF16), ones_blk), axis=0, keepdims=True))

        kmax_ref[...] = lax.fori_loop(0, n_chunks, body, jnp.zeros((1, LANES), F32))

    row = lax.broadcasted_iota(jnp.int32, (2 * HEAD_DIM, tq), 0)
    for m in range(2):
        if mode == "diff":
            q = q_ref[...]
            keep = (row >= HEAD_DIM) == (m == 1)
            grp = m
        else:
            q = jnp.concatenate([q_ref[m], q_ref[m]], axis=0)
            keep = (row >= HEAD_DIM) == (kv_head == 1)
            grp = kv_head
        qp = jnp.where(keep, q, jnp.zeros_like(q))
        qp_ref[m] = qp
        qf = qp.astype(F32)
        qn2 = jnp.sum(qf * qf, axis=0, keepdims=True)
        kmax2 = jnp.max(jnp.where(lane_grp == grp, kmax_ref[...], 0.0), axis=1, keepdims=True)
        mb_ref[m] = jnp.sqrt(qn2 * kmax2) * BOUND_SLACK
    acc_ref[...] = jnp.zeros(acc_ref.shape, F32)

    def fast_body(c, carry):
        k = k_chunk(c)
        vt = vt_ref[c]
        for m in range(2):
            p = jnp.exp2(_dot(k, qp_ref[m]) - mb_ref[m]).astype(BF16)
            acc_ref[m] += _dot(vt, p)
        return carry

    lax.fori_loop(0, n_chunks, fast_body, 0)

    denom_min = jnp.min(jnp.minimum(acc_ref[0, dv:dv + 1, :], acc_ref[1, dv:dv + 1, :]))

    @pl.when(jnp.logical_not(denom_min >= SAFE_DENOM))
    def _():
        mb_ref[...] = jnp.full(mb_ref.shape, NEG_BIG, F32)
        acc_ref[...] = jnp.zeros(acc_ref.shape, F32)

        def slow_body(c, carry):
            k = k_chunk(c)
            vt = vt_ref[c]
            for m in range(2):
                s = _dot(k, qp_ref[m])
                m_old = mb_ref[m]
                m_new = jnp.maximum(m_old, jnp.max(s, axis=0, keepdims=True))
                p = jnp.exp2(s - m_new).astype(BF16)
                acc_ref[m] = jnp.exp2(m_old - m_new) * acc_ref[m] + _dot(vt, p)
                mb_ref[m] = m_new
            return carry

        lax.fori_loop(0, n_chunks, slow_body, 0)

    a0, a1 = acc_ref[0], acc_ref[1]
    o0 = a0[:dv] * (1.0 / a0[dv:dv + 1])
    o1 = a1[:dv] * (1.0 / a1[dv:dv + 1])
    if mode == "diff":
        lam = (jnp.exp(jnp.sum(lq1_ref[...] * lk1_ref[...], keepdims=True))
               - jnp.exp(jnp.sum(lq2_ref[...] * lk2_ref[...], keepdims=True)) + B_LAMBDA_INIT)
        o = o0 - lam * o1
        ms = jnp.mean(o * o, axis=0, keepdims=True)
        o = o * lax.rsqrt(ms + RMS_EPS) * g_ref[...] * (1.0 - B_LAMBDA_INIT)
    else:
        o = jnp.concatenate([o0, o1], axis=0)
    o_ref[...] = o.T.astype(o_ref.dtype)


def _flash(mode, qt, k, vt, extra, tq):
    bsz, s = k.shape[0], k.shape[1]
    n_chunks, dve, ck = vt.shape[2:]
    dv = dve - ONES_ROWS
    if mode == "diff":
        q_spec = pl.BlockSpec((None, None, 2 * HEAD_DIM, tq), lambda b, h, qi: (b, h, 0, qi))
        k_spec = pl.BlockSpec((None, s, LANES), lambda b, h, qi: (b, 0, h))
        vt_spec = pl.BlockSpec((None, None, n_chunks, dve, ck), lambda b, h, qi: (b, h, 0, 0, 0))
        extra_specs = [_resident((1, HEAD_DIM))] * 4 + [_resident((B_V_DIM, 1))]
    else:
        q_spec = pl.BlockSpec((None, None, 2, HEAD_DIM, tq), lambda b, h, qi: (b, h, 0, 0, qi))
        k_spec = pl.BlockSpec((None, s, LANES), lambda b, h, qi: (b, 0, 0))
        vt_spec = pl.BlockSpec((None, None, n_chunks, dve, ck), lambda b, h, qi: (b, h // 2, 0, 0, 0))
        extra_specs = []
    return pl.pallas_call(
        functools.partial(_flash_kernel, mode=mode, dv=dv, tq=tq, ck=ck),
        grid=(bsz, 4, s // tq),
        in_specs=[q_spec, k_spec, vt_spec] + extra_specs,
        out_specs=pl.BlockSpec((None, tq, LANES), lambda b, h, qi: (b, qi, h)),
        out_shape=jax.ShapeDtypeStruct((bsz, s, 512), BF16),
        scratch_shapes=[pltpu.VMEM((2, 2 * HEAD_DIM, tq), BF16),
                        pltpu.VMEM((2, 1, tq), F32),
                        pltpu.VMEM((2, dve, tq), F32),
                        pltpu.VMEM((1, LANES), F32)],
        compiler_params=_params("parallel", "parallel", "arbitrary"),
        name="flash_" + mode,
    )(qt, k, vt, *extra)


def _q_transposed(x, n_heads):
    bsz, s, width = x.shape
    return x.reshape(bsz, s, n_heads, width // n_heads).transpose(0, 2, 3, 1)


def _v_transposed(x, n_heads, ck):
    bsz, s, width = x.shape
    xt = x.reshape(bsz, s // ck, ck, n_heads, width // n_heads).transpose(0, 3, 1, 4, 2)
    return jnp.concatenate([xt, jnp.ones((bsz, n_heads, s // ck, ONES_ROWS, ck), x.dtype)], axis=3)


def _dilated_kernel(q_ref, kp_ref, km_ref, kn_ref, vp_ref, vm_ref, vn_ref,
                    o_ref, m_ref, l_ref, *, tq, sub_len):
    i0 = pl.program_id(2) * tq
    q = q_ref[...]
    kw = jnp.concatenate([kp_ref[...], km_ref[...], kn_ref[...]], axis=0)
    vw = jnp.concatenate([vp_ref[...], vm_ref[...], vn_ref[...]], axis=0)
    nk = tq + 2 * LANES
    t = lax.broadcasted_iota(jnp.int32, (tq, nk), 0)
    w = lax.broadcasted_iota(jnp.int32, (tq, nk), 1)
    off = w - LANES - t
    j = i0 - LANES + w
    valid = (off >= -D_HALF_WINDOW) & (off <= D_HALF_WINDOW) & (j >= 0) & (j < sub_len)
    lane = lax.broadcasted_iota(jnp.int32, (1, D_WIDTH), 1)
    o_acc = jnp.zeros((tq, D_WIDTH), F32)
    m_acc = jnp.zeros((tq, D_WIDTH), F32)
    l_acc = jnp.zeros((tq, D_WIDTH), F32)
    for h in range(D_SLOTS):
        in_head = (lane >= h * HEAD_DIM) & (lane < (h + 1) * HEAD_DIM)
        qh = jnp.where(in_head, q, jnp.zeros_like(q))
        s = lax.dot_general(qh, kw, (((1,), (1,)), ((), ())), preferred_element_type=F32)
        s = jnp.where(valid, s, NEG_BIG)
        mh = jnp.max(s, axis=1, keepdims=True)
        p = jnp.exp2(s - mh)
        lh = jnp.sum(p, axis=1, keepdims=True)
        oh = _dot(p.astype(BF16), vw)
        sel = in_head.astype(F32)
        o_acc = o_acc + oh * sel
        m_acc = m_acc + mh * sel
        l_acc = l_acc + lh * sel
    o_ref[...] = o_acc
    m_ref[...] = m_acc
    l_ref[...] = l_acc


def _dilated(qd, kd, vd, pat, dilation):
    bsz, s, width = qd.shape
    sub_len = s // dilation
    tq = min(256, sub_len)
    n_col = width // D_WIDTH
    view = lambda a: a.reshape(bsz, sub_len, dilation * width)
    blk128 = sub_len // LANES
    per_tile = tq // LANES
    main = pl.BlockSpec((None, tq, D_WIDTH), lambda b, c, i: (b, i, c * n_col + pat))
    prev = pl.BlockSpec((None, LANES, D_WIDTH),
                        lambda b, c, i: (b, jnp.maximum(i * per_tile - 1, 0), c * n_col + pat))
    nxt = pl.BlockSpec((None, LANES, D_WIDTH),
                       lambda b, c, i: (b, jnp.minimum((i + 1) * per_tile, blk128 - 1), c * n_col + pat))
    out_spec = pl.BlockSpec((None, tq, D_WIDTH), lambda b, c, i: (b, i, c))
    out = jax.ShapeDtypeStruct((bsz, sub_len, dilation * D_WIDTH), F32)
    o, m, l = pl.pallas_call(
        functools.partial(_dilated_kernel, tq=tq, sub_len=sub_len),
        grid=(bsz, dilation, sub_len // tq),
        in_specs=[main, prev, main, nxt, prev, main, nxt],
        out_specs=[out_spec] * 3,
        out_shape=[out] * 3,
        compiler_params=_params("parallel", "parallel", "parallel"),
        name="dilated_p%d" % pat,
    )(view(qd), view(kd), view(kd), view(kd), view(vd), view(vd), view(vd))
    return tuple(a.reshape(bsz, s, D_WIDTH) for a in (o, m, l))


def _mix0_kernel(x_ref, u_ref, vn_ref, yb_ref, ws_ref, bs_ref, w_ref, g_ref, b_ref, o_ref, *, tm):
    lane = lax.broadcasted_iota(jnp.int32, (A_CHUNK, LANES), 1)
    left = lane < HEAD_DIM
    rows = []
    for n in range(tm // A_CHUNK):
        r = slice(n * A_CHUNK, (n + 1) * A_CHUNK)
        cols = []
        for j in range(A_WIDTH // LANES):
            v = vn_ref[r, j * LANES:(j + 1) * LANES]
            zero = jnp.zeros_like(v)
            stacked = jnp.concatenate([jnp.where(left, v, zero), jnp.where(left, zero, v)], axis=0)
            cols.append(_dot(ws_ref[j], stacked))
        rows.append(jnp.concatenate(cols, axis=1) + bs_ref[...])
    mixed = jnp.concatenate(rows, axis=0)
    ya = (u_ref[...] * mixed).astype(BF16)
    y = _dot(ya, w_ref[0:A_WIDTH, :]) + _dot(yb_ref[...], w_ref[A_WIDTH:, :])
    o_ref[...] = _layer_norm(ALPHA * x_ref[...] + y, g_ref[...], b_ref[...])


def _mix0(x, u, vn, yb, ws_pair, bs_x, w_out, g, b, tm):
    bsz, s, d = x.shape
    tok = lambda width: pl.BlockSpec((None, tm, width), lambda bb, i: (bb, i, 0))
    return pl.pallas_call(
        functools.partial(_mix0_kernel, tm=tm),
        grid=(bsz, s // tm),
        in_specs=[tok(d), tok(512), tok(512), tok(512), _resident(ws_pair.shape), _resident(bs_x.shape),
                  _resident(w_out.shape), _resident((1, d)), _resident((1, d))],
        out_specs=tok(d),
        out_shape=jax.ShapeDtypeStruct((bsz, s, d), F32),
        compiler_params=_params("parallel", "parallel"),
        name="mix0",
    )(x, u, vn, yb, ws_pair, bs_x, w_out, g, b)


def _mix1_kernel(x_ref, yc_ref, *refs):
    pat_refs, (w_ref, g_ref, b_ref, o_ref) = refs[:9], refs[9:]
    os_, ms_, ls_ = pat_refs[0::3], pat_refs[1::3], pat_refs[2::3]
    m_all = jnp.maximum(jnp.maximum(ms_[0][...], ms_[1][...]), ms_[2][...])
    num = jnp.zeros(m_all.shape, F32)
    den = jnp.zeros(m_all.shape, F32)
    for p in range(3):
        wgt = jnp.exp2(ms_[p][...] - m_all)
        num = num + wgt * os_[p][...]
        den = den + wgt * ls_[p][...]
    yd = (num * (1.0 / den)).astype(BF16)
    y = _dot(yc_ref[...], w_ref[0:512, :]) + _dot(yd, w_ref[512:, :])
    o_ref[...] = _layer_norm(ALPHA * x_ref[...] + y, g_ref[...], b_ref[...])


def _mix1(x, yc, pats, w_out, g, b, tm):
    bsz, s, d = x.shape
    tok = lambda width: pl.BlockSpec((None, tm, width), lambda bb, i: (bb, i, 0))
    flat = [a for oml in pats for a in oml]
    return pl.pallas_call(
        _mix1_kernel,
        grid=(bsz, s // tm),
        in_specs=[tok(d), tok(512)] + [tok(D_WIDTH)] * 9
                 + [_resident(w_out.shape), _resident((1, d)), _resident((1, d))],
        out_specs=tok(d),
        out_shape=jax.ShapeDtypeStruct((bsz, s, d), F32),
        compiler_params=_params("parallel", "parallel"),
        name="mix1",
    )(x, yc, *flat, w_out, g, b)


def _ffn_kernel(xp_ref, xm_ref, xn_ref, wa_ref, wg_ref, ca_ref, cg_ref, wd_ref, g_ref, b_ref,
                o_ref, xe_ref, acc_ref, *, tm, tiles_per_seq):
    i = pl.program_id(0) % tiles_per_seq
    halo = 8
    prev = jnp.where(i > 0, xp_ref[...], 0.0)
    nxt = jnp.where(i < tiles_per_seq - 1, xn_ref[...], 0.0)
    xe_ref[...] = jnp.concatenate([prev, xm_ref[...], nxt], axis=0).astype(BF16)
    acc_ref[...] = jnp.zeros(acc_ref.shape, F32)
    ext = tm + 2 * halo

    def conv(h, cw):
        mid = slice(halo, halo + tm)
        return (cw[0:1] * pltpu.roll(h, 1, 0)[mid] + cw[1:2] * h[mid]
                + cw[2:3] * pltpu.roll(h, ext - 1, 0)[mid] + cw[3:4])

    def body(c, carry):
        xe = xe_ref[...]
        a = conv(_dot(xe, wa_ref[c]), ca_ref[c])
        gt = conv(_dot(xe, wg_ref[c]), cg_ref[c])
        act = (jax.nn.gelu(gt) * a).astype(BF16)
        acc_ref[...] += _dot(act, wd_ref[c])
        return carry

    lax.fori_loop(0, wa_ref.shape[0], body, 0)
    o_ref[...] = _layer_norm(ALPHA * xm_ref[...] + acc_ref[...], g_ref[...], b_ref[...])


def _ffn(x, wa, wg, ca, cg, wd, g, b, tm):
    bsz, s, d = x.shape
    n_tok = bsz * s
    xf = x.reshape(n_tok, d)
    tiles_per_seq = s // tm
    r8 = tm // 8
    n8 = n_tok // 8
    out = pl.pallas_call(
        functools.partial(_ffn_kernel, tm=tm, tiles_per_seq=tiles_per_seq),
        grid=(n_tok // tm,),
        in_specs=[pl.BlockSpec((8, d), lambda i: (jnp.maximum(i * r8 - 1, 0), 0)),
                  pl.BlockSpec((tm, d), lambda i: (i, 0)),
                  pl.BlockSpec((8, d), lambda i: (jnp.minimum((i + 1) * r8, n8 - 1), 0)),
                  _resident(wa.shape), _resident(wg.shape), _resident(ca.shape), _resident(cg.shape),
                  _resident(wd.shape), _resident((1, d)), _resident((1, d))],
        out_specs=pl.BlockSpec((tm, d), lambda i: (i, 0)),
        out_shape=jax.ShapeDtypeStruct((n_tok, d), F32),
        scratch_shapes=[pltpu.VMEM((tm + 16, d), BF16), pltpu.VMEM((tm, d), F32)],
        compiler_params=_params("parallel"),
        name="conv_ffn",
    )(xf, xf, xf, wa, wg, ca, cg, wd, g, b)
    return out.reshape(bsz, s, d)


def _rope_tables(pos, n_dims, theta):
    inv = jnp.power(theta, -jnp.arange(0, n_dims, 2, dtype=F32) / n_dims)
    ang = pos.astype(F32)[:, None] * inv[None, :]
    return jnp.cos(ang), jnp.sin(ang)


def _lane_tables(parts, s):
    cs, lo, hi = [], [], []
    zero = None
    for cos, sin in parts:
        zero = jnp.zeros_like(sin)
        cs += [cos, cos]
        lo += [-sin, zero]
        hi += [zero, sin]
    used = sum(a.shape[1] for a in cs)
    pad1 = jnp.ones((s, HEAD_DIM - used), F32)
    pad0 = jnp.zeros((s, HEAD_DIM - used), F32)
    build = lambda xs, pad: jnp.tile(jnp.concatenate(xs + [pad], axis=1), (1, LANES // HEAD_DIM))
    return build(cs, pad1), build(lo, pad0), build(hi, pad0)


def _ffn_weights(w_up, conv_w, conv_b, w_down):
    d = w_up.shape[0]
    n_c = D_FF // FF_CHUNK
    chunks = lambda w: w.reshape(d, n_c, FF_CHUNK).transpose(1, 0, 2).astype(BF16)
    wa, wg = chunks(w_up[:, :D_FF]), chunks(w_up[:, D_FF:])

    def conv_pack(cw, cb):
        rows = jnp.concatenate([cw, cb[None, :], jnp.zeros((4, D_FF), F32)], axis=0)
        return rows.reshape(8, n_c, FF_CHUNK).transpose(1, 0, 2)

    ca = conv_pack(conv_w[:, :D_FF], conv_b[:D_FF])
    cg = conv_pack(conv_w[:, D_FF:], conv_b[D_FF:])
    wd = w_down.reshape(n_c, FF_CHUNK, d).astype(BF16)
    return wa, wg, ca, cg, wd


def _trunk(x, p, tm=512, tq=512):
    bsz, s, d = x.shape
    tm, tq, ck = min(tm, s), min(tq, s), min(KV_CHUNK, s)
    pos = jnp.arange(s)
    tabs_p = _lane_tables([_rope_tables(pos, ROPE_DIMS, ROPE_THETA)], s)
    tabs_a = _lane_tables([_rope_tables(pos // GRID_W, HEAD_DIM // 2, AXIAL_THETA),
                           _rope_tables(pos % GRID_W, HEAD_DIM // 2, AXIAL_THETA)], s)
    row = lambda v: v.reshape(1, -1)

    u, vn, q0, k0, v0 = _inproj0(x, p["w_in0"], row(p["a_ln_g"]), row(p["a_ln_b"]), tabs_p, tm)
    yb = _flash("diff", _q_transposed(q0, B_HEADS), k0, _v_transposed(v0, B_HEADS, ck),
                [row(p["b_lam_q1"]), row(p["b_lam_k1"]), row(p["b_lam_q2"]), row(p["b_lam_k2"]),
                 p["b_subln_g"].reshape(-1, 1)], tq)
    x = _mix0(x, u, vn, yb, p["ws_pair"], p["bs_x"], p["w_out0"], row(p["ln0a_g"]), row(p["ln0a_b"]), tm)
    x = _ffn(x, *p["ffn0"], row(p["ln0b_g"]), row(p["ln0b_b"]), tm)

    qc, kc, vc, qd, kd, vd = _inproj1(x, p["w_in1"], p["qg"], p["kg"], p["ones_blk"], tabs_p, tabs_a, tm)
    yc = _flash("gqa", _q_transposed(qc, C_Q_HEADS).reshape(bsz, 4, 2, HEAD_DIM, s), kc,
                _v_transposed(vc, C_KV_HEADS, ck), [], tq)
    pats = [_dilated(qd, kd, vd, pi, dil) for pi, (_, dil) in enumerate(D_PATTERNS)]
    x = _mix1(x, yc, pats, p["w_out1"], row(p["ln1a_g"]), row(p["ln1a_b"]), tm)
    x = _ffn(x, *p["ffn1"], row(p["ln1b_g"]), row(p["ln1b_b"]), tm)
    return x


def _prepare(w_in0, a_ln_g, a_ln_b, a_ws, a_bs, b_lam_q1, b_lam_k1, b_lam_q2, b_lam_k2, b_subln_g,
             w_out0, ln0a_g, ln0a_b, w_up0, conv_w0, conv_b0, w_down0, ln0b_g, ln0b_b,
             w_in1, c_qnorm_g, c_knorm_g, w_out1, ln1a_g, ln1a_b, w_up1, conv_w1, conv_b1, w_down1,
             ln1b_g, ln1b_b):
    grp = jnp.arange(LANES) // HEAD_DIM
    return dict(
        w_in0=w_in0.astype(BF16), a_ln_g=a_ln_g, a_ln_b=a_ln_b,
        ws_pair=a_ws.reshape(A_GROUPS // 2, 2, A_CHUNK, A_CHUNK).transpose(0, 2, 1, 3)
        .reshape(A_GROUPS // 2, A_CHUNK, 2 * A_CHUNK).astype(BF16),
        bs_x=jnp.repeat(a_bs.T, HEAD_DIM, axis=1),
        b_lam_q1=b_lam_q1, b_lam_k1=b_lam_k1, b_lam_q2=b_lam_q2, b_lam_k2=b_lam_k2, b_subln_g=b_subln_g,
        w_out0=w_out0.astype(BF16), ln0a_g=ln0a_g, ln0a_b=ln0a_b,
        ffn0=_ffn_weights(w_up0, conv_w0, conv_b0, w_down0), ln0b_g=ln0b_g, ln0b_b=ln0b_b,
        w_in1=w_in1.astype(BF16),
        qg=jnp.tile(c_qnorm_g, LANES // HEAD_DIM).reshape(1, LANES),
        kg=jnp.tile(c_knorm_g, LANES // HEAD_DIM).reshape(1, LANES),
        ones_blk=(grp[:, None] == grp[None, :]).astype(BF16),
        w_out1=w_out1.astype(BF16), ln1a_g=ln1a_g, ln1a_b=ln1a_b,
        ffn1=_ffn_weights(w_up1, conv_w1, conv_b1, w_down1), ln1b_g=ln1b_g, ln1b_b=ln1b_b,
    )


def kernel(x_prompt, x_sample, w_in0, a_ln_g, a_ln_b, a_ws, a_bs, b_lam_q1, b_lam_k1, b_lam_q2, b_lam_k2, b_subln_g, w_out0, ln0a_g, ln0a_b, w_up0, conv_w0, conv_b0, w_down0, ln0b_g, ln0b_b, w_in1, c_qnorm_g, c_knorm_g, w_out1, ln1a_g, ln1a_b, w_up1, conv_w1, conv_b1, w_down1, ln1b_g, ln1b_b):
    p = _prepare(w_in0, a_ln_g, a_ln_b, a_ws, a_bs, b_lam_q1, b_lam_k1, b_lam_q2, b_lam_k2, b_subln_g,
                 w_out0, ln0a_g, ln0a_b, w_up0, conv_w0, conv_b0, w_down0, ln0b_g, ln0b_b,
                 w_in1, c_qnorm_g, c_knorm_g, w_out1, ln1a_g, ln1a_b, w_up1, conv_w1, conv_b1, w_down1,
                 ln1b_g, ln1b_b)
    return (_trunk(x_prompt, p), _trunk(x_sample, p))
```

```python
import functools
import math

import jax
import jax.numpy as jnp
from jax import lax
from jax.experimental import pallas as pl
from jax.experimental.pallas import tpu as pltpu

F32 = jnp.float32
BF16 = jnp.bfloat16

D_MODEL = 1024
DEPTH = 2
HEAD_DIM = 64
GRID_W = 64
LN_EPS = 1e-5
RMS_EPS = 1e-6
ROPE_THETA = 500000.0
ROPE_DIMS = HEAD_DIM // 4
AXIAL_THETA = 10000.0
A_GROUPS = 8
A_WIDTH = A_GROUPS * HEAD_DIM
A_CHUNK = 128
B_HEADS = 4
B_V_DIM = 2 * HEAD_DIM
B_LAMBDA_INIT = 0.8 - 0.6 * math.exp(-0.3 * 0)
C_Q_HEADS = 8
C_KV_HEADS = 2
D_PATTERNS = ((128, 1), (512, 4), (2048, 16))
D_SLOTS = 4
D_WIDTH = D_SLOTS * HEAD_DIM
D_HALF_WINDOW = 64
D_FF = 2816
FF_CHUNK = 256
FF_DOWN_GROUP = 4
ALPHA = (2 * DEPTH) ** 0.25
QK_SCALE = HEAD_DIM ** -0.5 * math.log2(math.e)
NEG_BIG = -1e30
BOUND_SLACK = 1.0 + 2.0 ** -6
SAFE_DENOM = 2.0 ** -64
KV_CHUNK = 1024

LANES = 128
ONES_ROWS = 16
VMEM_LIMIT = 56 * 1024 * 1024


def _params(*sem):
    return pltpu.CompilerParams(dimension_semantics=sem, vmem_limit_bytes=VMEM_LIMIT)


def _resident(shape):
    nd = len(shape)
    return pl.BlockSpec(shape, lambda *_: (0,) * nd, pipeline_mode=pl.Buffered(1))


def _layer_norm(x, g, b):
    mu = jnp.mean(x, -1, keepdims=True)
    xc = x - mu
    var = jnp.mean(xc * xc, -1, keepdims=True)
    return xc * lax.rsqrt(var + LN_EPS) * g + b


def _rope(x, c, s_lo, s_hi, shift):
    return (x * c + pltpu.roll(x, LANES - shift, 1) * s_lo + pltpu.roll(x, shift, 1) * s_hi)


def _dot(a, b):
    return jnp.dot(a, b, preferred_element_type=F32)


def _inproj0_kernel(x_ref, w_ref, lng_ref, lnb_ref, c_ref, s1_ref, s2_ref,
                    u_ref, vn_ref, q_ref, k_ref, vb_ref):
    xb = x_ref[...].astype(BF16)
    za = jax.nn.gelu(_dot(xb, w_ref[:, 0:2 * A_WIDTH]))
    u_ref[...] = za[:, :A_WIDTH]
    vn_ref[...] = _layer_norm(za[:, A_WIDTH:], lng_ref[...], lnb_ref[...]).astype(BF16)
    c, s1, s2 = c_ref[...], s1_ref[...], s2_ref[...]
    base = 2 * A_WIDTH
    hq = _dot(xb, w_ref[:, base:base + 512])
    hk = _dot(xb, w_ref[:, base + 512:base + 1024])
    for j in range(4):
        sl = slice(j * LANES, (j + 1) * LANES)
        q_ref[:, sl] = (_rope(hq[:, sl], c, s1, s2, ROPE_DIMS // 2) * QK_SCALE).astype(BF16)
        k_ref[:, sl] = _rope(hk[:, sl], c, s1, s2, ROPE_DIMS // 2).astype(BF16)
    vb_ref[...] = _dot(xb, w_ref[:, base + 1024:base + 1536]).astype(BF16)


def _inproj0(x, w, lng, lnb, tabs, tm):
    bsz, s, d = x.shape
    n_in = w.shape[1]
    tok = lambda width: pl.BlockSpec((None, tm, width), lambda b, i: (b, i, 0))
    tab = pl.BlockSpec((tm, LANES), lambda b, i: (i, 0))
    outs = [jax.ShapeDtypeStruct((bsz, s, 512), F32)] + [jax.ShapeDtypeStruct((bsz, s, 512), BF16)] * 4
    return pl.pallas_call(
        _inproj0_kernel,
        grid=(bsz, s // tm),
        in_specs=[tok(d), _resident((d, n_in)), _resident((1, 512)), _resident((1, 512)), tab, tab, tab],
        out_specs=[tok(512)] * 5,
        out_shape=outs,
        compiler_params=_params("parallel", "parallel"),
        name="inproj0",
    )(x, w, lng, lnb, *tabs)


def _group_rms(x, gain, ones_blk):
    x2 = x * x
    hi = x2.astype(BF16)
    lo = (x2 - hi.astype(F32)).astype(BF16)
    ss = _dot(hi, ones_blk) + _dot(lo, ones_blk)
    return x * lax.rsqrt(ss * (1.0 / HEAD_DIM) + RMS_EPS) * gain


def _inproj1_kernel(x_ref, w_ref, qg_ref, kg_ref, ones_ref, c_ref, s1_ref, s2_ref,
                    ca_ref, sa1_ref, sa2_ref,
                    qc_ref, kc_ref, vc_ref, *rest, tm):
    d_refs, scr_refs = rest[:9], rest[9:]

    def emit_classes(kind, vals):
        for pi, (_, dil) in enumerate(D_PATTERNS):
            out_ref = d_refs[3 * pi + kind]
            blk = vals[:, pi * D_WIDTH:(pi + 1) * D_WIDTH]
            if dil == 1:
                out_ref[0] = blk.astype(BF16)
                continue
            scr = scr_refs[3 * (pi - 1) + kind]
            for j in range(D_WIDTH // LANES):
                scr[j] = blk[:, j * LANES:(j + 1) * LANES]
            for c in range(dil):
                out_ref[c] = jnp.concatenate(
                    [scr[j, pl.ds(c, tm // dil, stride=dil), :] for j in range(D_WIDTH // LANES)],
                    axis=1).astype(BF16)

    xb = x_ref[...].astype(BF16)
    ones_blk = ones_ref[...]
    ca, sa1, sa2 = ca_ref[...], sa1_ref[...], sa2_ref[...]
    hq = _dot(xb, w_ref[:, 0:512])
    for j in range(4):
        sl = slice(j * LANES, (j + 1) * LANES)
        qn = _group_rms(hq[:, sl], qg_ref[...], ones_blk)
        qc_ref[:, sl] = (_rope(qn, ca, sa1, sa2, HEAD_DIM // 4) * QK_SCALE).astype(BF16)
    hkv = _dot(xb, w_ref[:, 512:768])
    kn = _group_rms(hkv[:, :LANES], kg_ref[...], ones_blk)
    kc_ref[...] = _rope(kn, ca, sa1, sa2, HEAD_DIM // 4).astype(BF16)
    vc_ref[...] = hkv[:, LANES:].astype(BF16)
    c, s1, s2 = c_ref[...], s1_ref[...], s2_ref[...]
    hqd = _dot(xb, w_ref[:, 768:1536])
    hkd = _dot(xb, w_ref[:, 1536:2304])
    blocks = range(3 * D_WIDTH // LANES)
    lane_block = lambda j: slice(j * LANES, (j + 1) * LANES)
    emit_classes(0, jnp.concatenate(
        [_rope(hqd[:, lane_block(j)], c, s1, s2, ROPE_DIMS // 2) * QK_SCALE for j in blocks], axis=1))
    emit_classes(1, jnp.concatenate(
        [_rope(hkd[:, lane_block(j)], c, s1, s2, ROPE_DIMS // 2) for j in blocks], axis=1))
    emit_classes(2, _dot(xb, w_ref[:, 2304:3072]))


def _inproj1(x, w, qg, kg, ones_blk, tabs_p, tabs_a, tm):
    bsz, s, d = x.shape
    n_in = w.shape[1]
    tok = lambda width: pl.BlockSpec((None, tm, width), lambda b, i: (b, i, 0))
    tab = pl.BlockSpec((tm, LANES), lambda b, i: (i, 0))
    widths = (512, 128, 128)
    out_specs = [tok(wd) for wd in widths]
    out_shape = [jax.ShapeDtypeStruct((bsz, s, wd), BF16) for wd in widths]
    scratch = []
    for _, dil in D_PATTERNS:
        out_specs += [pl.BlockSpec((None, dil, tm // dil, D_WIDTH), lambda b, i: (b, 0, i, 0))] * 3
        out_shape += [jax.ShapeDtypeStruct((bsz, dil, s // dil, D_WIDTH), BF16)] * 3
        if dil > 1:
            scratch += [pltpu.VMEM((D_WIDTH // LANES, tm, LANES), F32)] * 3
    return pl.pallas_call(
        functools.partial(_inproj1_kernel, tm=tm),
        grid=(bsz, s // tm),
        in_specs=[tok(d), _resident((d, n_in)), _resident((1, LANES)), _resident((1, LANES)),
                  _resident((LANES, LANES))] + [tab] * 6,
        out_specs=out_specs,
        out_shape=out_shape,
        scratch_shapes=scratch,
        compiler_params=_params("parallel", "parallel"),
        name="inproj1",
    )(x, w, qg, kg, ones_blk, *tabs_p, *tabs_a)


def _flash_kernel(*refs, mode, dv, tq, ck):
    if mode == "diff":
        (q_ref, k_ref, vt_ref, lq1_ref, lk1_ref, lq2_ref, lk2_ref, g_ref,
         o_ref, qp_ref, mb_ref, acc_ref, kmax_ref) = refs
    else:
        q_ref, k_ref, vt_ref, o_ref, qp_ref, mb_ref, acc_ref, kmax_ref = refs
    n_chunks = vt_ref.shape[0]
    kv_head = pl.program_id(1) // 2
    lane_grp = lax.broadcasted_iota(jnp.int32, (1, LANES), 1) // HEAD_DIM

    def k_chunk(c):
        return k_ref[pl.ds(pl.multiple_of(c * ck, ck), ck), :]

    @pl.when(pl.program_id(2) == 0)
    def _():
        r = lax.broadcasted_iota(jnp.int32, (LANES, LANES), 0) // HEAD_DIM
        c_ = lax.broadcasted_iota(jnp.int32, (LANES, LANES), 1) // HEAD_DIM
        ones_blk = (r == c_).astype(BF16)

        def body(c, mx):
            kf = k_chunk(c).astype(F32)
            return jnp.maximum(mx, jnp.max(_dot((kf * kf).astype(BF16), ones_blk), axis=0, keepdims=True))

        kmax_ref[...] = lax.fori_loop(0, n_chunks, body, jnp.zeros((1, LANES), F32))

    row = lax.broadcasted_iota(jnp.int32, (2 * HEAD_DIM, tq), 0)
    for m in range(2):
        if mode == "diff":
            q = q_ref[...]
            keep = (row >= HEAD_DIM) == (m == 1)
            grp = m
        else:
            q = jnp.concatenate([q_ref[m], q_ref[m]], axis=0)
            keep = (row >= HEAD_DIM) == (kv_head == 1)
            grp = kv_head
        qp = jnp.where(keep, q, jnp.zeros_like(q))
        qp_ref[m] = qp
        qf = qp.astype(F32)
        qn2 = jnp.sum(qf * qf, axis=0, keepdims=True)
        kmax2 = jnp.max(jnp.where(lane_grp == grp, kmax_ref[...], 0.0), axis=1, keepdims=True)
        mb_ref[m] = jnp.sqrt(qn2 * kmax2) * BOUND_SLACK
    acc_ref[...] = jnp.zeros(acc_ref.shape, F32)

    def fast_body(c, carry):
        k = k_chunk(c)
        vt = vt_ref[c]
        for m in range(2):
            p = jnp.exp2(_dot(k, qp_ref[m]) - mb_ref[m]).astype(BF16)
            acc_ref[m] += _dot(vt, p)
        return carry

    lax.fori_loop(0, n_chunks, fast_body, 0)

    denom_min = jnp.min(jnp.minimum(acc_ref[0, dv:dv + 1, :], acc_ref[1, dv:dv + 1, :]))

    @pl.when(jnp.logical_not(denom_min >= SAFE_DENOM))
    def _():
        mb_ref[...] = jnp.full(mb_ref.shape, NEG_BIG, F32)
        acc_ref[...] = jnp.zeros(acc_ref.shape, F32)

        def slow_body(c, carry):
            k = k_chunk(c)
            vt = vt_ref[c]
            for m in range(2):
                s = _dot(k, qp_ref[m])
                m_old = mb_ref[m]
                m_new = jnp.maximum(m_old, jnp.max(s, axis=0, keepdims=True))
                p = jnp.exp2(s - m_new).astype(BF16)
                acc_ref[m] = jnp.exp2(m_old - m_new) * acc_ref[m] + _dot(vt, p)
                mb_ref[m] = m_new
            return carry

        lax.fori_loop(0, n_chunks, slow_body, 0)

    a0, a1 = acc_ref[0], acc_ref[1]
    o0 = a0[:dv] * (1.0 / a0[dv:dv + 1])
    o1 = a1[:dv] * (1.0 / a1[dv:dv + 1])
    if mode == "diff":
        lam = (jnp.exp(jnp.sum(lq1_ref[...] * lk1_ref[...], keepdims=True))
               - jnp.exp(jnp.sum(lq2_ref[...] * lk2_ref[...], keepdims=True)) + B_LAMBDA_INIT)
        o = o0 - lam * o1
        ms = jnp.mean(o * o, axis=0, keepdims=True)
        o = o * lax.rsqrt(ms + RMS_EPS) * g_ref[...] * (1.0 - B_LAMBDA_INIT)
    else:
        o = jnp.concatenate([o0, o1], axis=0)
    o_ref[...] = o.T.astype(o_ref.dtype)


def _flash(mode, qt, k, vt, extra, tq):
    bsz, s = k.shape[0], k.shape[1]
    n_chunks, dve, ck = vt.shape[2:]
    dv = dve - ONES_ROWS
    if mode == "diff":
        q_spec = pl.BlockSpec((None, None, 2 * HEAD_DIM, tq), lambda b, h, qi: (b, h, 0, qi))
        k_spec = pl.BlockSpec((None, s, LANES), lambda b, h, qi: (b, 0, h))
        vt_spec = pl.BlockSpec((None, None, n_chunks, dve, ck), lambda b, h, qi: (b, h, 0, 0, 0))
        extra_specs = [_resident((1, HEAD_DIM))] * 4 + [_resident((B_V_DIM, 1))]
    else:
        q_spec = pl.BlockSpec((None, None, 2, HEAD_DIM, tq), lambda b, h, qi: (b, h, 0, 0, qi))
        k_spec = pl.BlockSpec((None, s, LANES), lambda b, h, qi: (b, 0, 0))
        vt_spec = pl.BlockSpec((None, None, n_chunks, dve, ck), lambda b, h, qi: (b, h // 2, 0, 0, 0))
        extra_specs = []
    return pl.pallas_call(
        functools.partial(_flash_kernel, mode=mode, dv=dv, tq=tq, ck=ck),
        grid=(bsz, 4, s // tq),
        in_specs=[q_spec, k_spec, vt_spec] + extra_specs,
        out_specs=pl.BlockSpec((None, tq, LANES), lambda b, h, qi: (b, qi, h)),
        out_shape=jax.ShapeDtypeStruct((bsz, s, 512), BF16),
        scratch_shapes=[pltpu.VMEM((2, 2 * HEAD_DIM, tq), BF16),
                        pltpu.VMEM((2, 1, tq), F32),
                        pltpu.VMEM((2, dve, tq), F32),
                        pltpu.VMEM((1, LANES), F32)],
        compiler_params=_params("parallel", "parallel", "arbitrary"),
        name="flash_" + mode,
    )(qt, k, vt, *extra)


def _q_transposed(x, n_heads):
    bsz, s, width = x.shape
    return x.reshape(bsz, s, n_heads, width // n_heads).transpose(0, 2, 3, 1)


def _v_transposed(x, n_heads, ck):
    bsz, s, width = x.shape
    xt = x.reshape(bsz, s // ck, ck, n_heads, width // n_heads).transpose(0, 3, 1, 4, 2)
    return jnp.concatenate([xt, jnp.ones((bsz, n_heads, s // ck, ONES_ROWS, ck), x.dtype)], axis=3)


def _dilated_kernel(q_ref, kp_ref, km_ref, kn_ref, vp_ref, vm_ref, vn_ref,
                    o_ref, m_ref, l_ref, *, tq, sub_len):
    i0 = pl.program_id(2) * tq
    q = q_ref[...]
    kw = jnp.concatenate([kp_ref[...], km_ref[...], kn_ref[...]], axis=0)
    vw = jnp.concatenate([vp_ref[...], vm_ref[...], vn_ref[...]], axis=0)
    nk = tq + 2 * LANES
    t = lax.broadcasted_iota(jnp.int32, (tq, nk), 0)
    w = lax.broadcasted_iota(jnp.int32, (tq, nk), 1)
    off = w - LANES - t
    j = i0 - LANES + w
    valid = (off >= -D_HALF_WINDOW) & (off <= D_HALF_WINDOW) & (j >= 0) & (j < sub_len)
    lane = lax.broadcasted_iota(jnp.int32, (1, D_WIDTH), 1)
    o_acc = jnp.zeros((tq, D_WIDTH), F32)
    m_acc = jnp.zeros((tq, D_WIDTH), F32)
    l_acc = jnp.zeros((tq, D_WIDTH), F32)
    for h in range(D_SLOTS):
        in_head = (lane >= h * HEAD_DIM) & (lane < (h + 1) * HEAD_DIM)
        qh = jnp.where(in_head, q, jnp.zeros_like(q))
        s = lax.dot_general(qh, kw, (((1,), (1,)), ((), ())), preferred_element_type=F32)
        s = jnp.where(valid, s, NEG_BIG)
        mh = jnp.max(s, axis=1, keepdims=True)
        p = jnp.exp2(s - mh)
        lh = jnp.sum(p, axis=1, keepdims=True)
        oh = _dot(p.astype(BF16), vw)
        sel = in_head.astype(F32)
        o_acc = o_acc + oh * sel
        m_acc = m_acc + mh * sel
        l_acc = l_acc + lh * sel
    o_ref[...] = o_acc
    m_ref[...] = m_acc
    l_ref[...] = l_acc


def _dilated(qd, kd, vd, pat):
    bsz, dilation, sub_len, _ = qd.shape
    tq = min(256, sub_len)
    blk128 = sub_len // LANES
    per_tile = tq // LANES
    main = pl.BlockSpec((None, None, tq, D_WIDTH), lambda b, c, i: (b, c, i, 0))
    prev = pl.BlockSpec((None, None, LANES, D_WIDTH),
                        lambda b, c, i: (b, c, jnp.maximum(i * per_tile - 1, 0), 0))
    nxt = pl.BlockSpec((None, None, LANES, D_WIDTH),
                       lambda b, c, i: (b, c, jnp.minimum((i + 1) * per_tile, blk128 - 1), 0))
    out = jax.ShapeDtypeStruct((bsz, dilation, sub_len, D_WIDTH), F32)
    return pl.pallas_call(
        functools.partial(_dilated_kernel, tq=tq, sub_len=sub_len),
        grid=(bsz, dilation, sub_len // tq),
        in_specs=[main, prev, main, nxt, prev, main, nxt],
        out_specs=[main] * 3,
        out_shape=[out] * 3,
        compiler_params=_params("parallel", "parallel", "parallel"),
        name="dilated_p%d" % pat,
    )(qd, kd, kd, kd, vd, vd, vd)


def _mix0_kernel(x_ref, u_ref, vn_ref, yb_ref, ws_ref, bs_ref, w_ref, g_ref, b_ref, o_ref, *, tm):
    lane = lax.broadcasted_iota(jnp.int32, (A_CHUNK, LANES), 1)
    left = lane < HEAD_DIM
    rows = []
    for n in range(tm // A_CHUNK):
        r = slice(n * A_CHUNK, (n + 1) * A_CHUNK)
        cols = []
        for j in range(A_WIDTH // LANES):
            v = vn_ref[r, j * LANES:(j + 1) * LANES]
            zero = jnp.zeros_like(v)
            stacked = jnp.concatenate([jnp.where(left, v, zero), jnp.where(left, zero, v)], axis=0)
            cols.append(_dot(ws_ref[j], stacked))
        rows.append(jnp.concatenate(cols, axis=1) + bs_ref[...])
    mixed = jnp.concatenate(rows, axis=0)
    ya = (u_ref[...] * mixed).astype(BF16)
    y = _dot(ya, w_ref[0:A_WIDTH, :]) + _dot(yb_ref[...], w_ref[A_WIDTH:, :])
    o_ref[...] = _layer_norm(ALPHA * x_ref[...] + y, g_ref[...], b_ref[...])


def _mix0(x, u, vn, yb, ws_pair, bs_x, w_out, g, b, tm):
    bsz, s, d = x.shape
    tok = lambda width: pl.BlockSpec((None, tm, width), lambda bb, i: (bb, i, 0))
    return pl.pallas_call(
        functools.partial(_mix0_kernel, tm=tm),
        grid=(bsz, s // tm),
        in_specs=[tok(d), tok(512), tok(512), tok(512), _resident(ws_pair.shape), _resident(bs_x.shape),
                  _resident(w_out.shape), _resident((1, d)), _resident((1, d))],
        out_specs=tok(d),
        out_shape=jax.ShapeDtypeStruct((bsz, s, d), F32),
        compiler_params=_params("parallel", "parallel"),
        name="mix0",
    )(x, u, vn, yb, ws_pair, bs_x, w_out, g, b)


def _mix1_kernel(x_ref, yc_ref, *refs, tm):
    pat_refs, (w_ref, g_ref, b_ref, o_ref), scr_refs = refs[:9], refs[9:13], refs[13:]

    def token_major(idx):
        dil = D_PATTERNS[idx // 3][1]
        ref = pat_refs[idx]
        if dil == 1:
            return ref[0]
        scr = scr_refs[idx - 3]
        for c in range(dil):
            for j in range(D_WIDTH // LANES):
                scr[j, pl.ds(c, tm // dil, stride=dil), :] = ref[c, :, j * LANES:(j + 1) * LANES]
        return jnp.concatenate([scr[j] for j in range(D_WIDTH // LANES)], axis=1)

    vals = [token_major(i) for i in range(9)]
    os_, ms_, ls_ = vals[0::3], vals[1::3], vals[2::3]
    m_all = jnp.maximum(jnp.maximum(ms_[0], ms_[1]), ms_[2])
    num = jnp.zeros(m_all.shape, F32)
    den = jnp.zeros(m_all.shape, F32)
    for p in range(3):
        wgt = jnp.exp2(ms_[p] - m_all)
        num = num + wgt * os_[p]
        den = den + wgt * ls_[p]
    yd = (num * (1.0 / den)).astype(BF16)
    y = _dot(yc_ref[...], w_ref[0:512, :]) + _dot(yd, w_ref[512:, :])
    o_ref[...] = _layer_norm(ALPHA * x_ref[...] + y, g_ref[...], b_ref[...])


def _mix1(x, yc, pats, w_out, g, b, tm):
    bsz, s, d = x.shape
    tok = lambda width: pl.BlockSpec((None, tm, width), lambda bb, i: (bb, i, 0))
    flat = [a for oml in pats for a in oml]
    pat_specs = [pl.BlockSpec((None, a.shape[1], tm // a.shape[1], D_WIDTH), lambda bb, i: (bb, 0, i, 0))
                 for a in flat]
    return pl.pallas_call(
        functools.partial(_mix1_kernel, tm=tm),
        grid=(bsz, s // tm),
        in_specs=[tok(d), tok(512)] + pat_specs
                 + [_resident(w_out.shape), _resident((1, d)), _resident((1, d))],
        out_specs=tok(d),
        out_shape=jax.ShapeDtypeStruct((bsz, s, d), F32),
        scratch_shapes=[pltpu.VMEM((D_WIDTH // LANES, tm, LANES), F32)] * 6,
        compiler_params=_params("parallel", "parallel"),
        name="mix1",
    )(x, yc, *flat, w_out, g, b)


def _ffn_kernel(xp_ref, xm_ref, xn_ref, wa_ref, wg_ref, ca_ref, cg_ref, wd_ref, g_ref, b_ref,
                o_ref, xe_ref, *, tm, tiles_per_seq):
    i = pl.program_id(0) % tiles_per_seq
    halo = 8
    prev = jnp.where(i > 0, xp_ref[...], 0.0)
    nxt = jnp.where(i < tiles_per_seq - 1, xn_ref[...], 0.0)
    xe_ref[...] = jnp.concatenate([prev, xm_ref[...], nxt], axis=0).astype(BF16)
    ext = tm + 2 * halo

    def conv(h, cw):
        mid = slice(halo, halo + tm)
        return (cw[0:1] * pltpu.roll(h, 1, 0)[mid] + cw[1:2] * h[mid]
                + cw[2:3] * pltpu.roll(h, ext - 1, 0)[mid] + cw[3:4])

    n_c = wa_ref.shape[0]
    y = None
    acts = []
    for c in range(n_c):
        xe = xe_ref[...]
        a = conv(_dot(xe, wa_ref[c]), ca_ref[c])
        gt = conv(_dot(xe, wg_ref[c]), cg_ref[c])
        acts.append((jax.nn.gelu(gt) * a).astype(BF16))
        if len(acts) == FF_DOWN_GROUP or c == n_c - 1:
            c0 = c + 1 - len(acts)
            part = _dot(jnp.concatenate(acts, axis=1), wd_ref[c0 * FF_CHUNK:(c + 1) * FF_CHUNK, :])
            y = part if y is None else y + part
            acts = []
    o_ref[...] = _layer_norm(ALPHA * xm_ref[...] + y, g_ref[...], b_ref[...])


def _ffn(x, wa, wg, ca, cg, wd, g, b, tm):
    bsz, s, d = x.shape
    n_tok = bsz * s
    xf = x.reshape(n_tok, d)
    tiles_per_seq = s // tm
    r8 = tm // 8
    n8 = n_tok // 8
    out = pl.pallas_call(
        functools.partial(_ffn_kernel, tm=tm, tiles_per_seq=tiles_per_seq),
        grid=(n_tok // tm,),
        in_specs=[pl.BlockSpec((8, d), lambda i: (jnp.maximum(i * r8 - 1, 0), 0)),
                  pl.BlockSpec((tm, d), lambda i: (i, 0)),
                  pl.BlockSpec((8, d), lambda i: (jnp.minimum((i + 1) * r8, n8 - 1), 0)),
                  _resident(wa.shape), _resident(wg.shape), _resident(ca.shape), _resident(cg.shape),
                  _resident(wd.shape), _resident((1, d)), _resident((1, d))],
        out_specs=pl.BlockSpec((tm, d), lambda i: (i, 0)),
        out_shape=jax.ShapeDtypeStruct((n_tok, d), F32),
        scratch_shapes=[pltpu.VMEM((tm + 16, d), BF16)],
        compiler_params=_params("parallel"),
        name="conv_ffn",
    )(xf, xf, xf, wa, wg, ca, cg, wd, g, b)
    return out.reshape(bsz, s, d)


def _rope_tables(pos, n_dims, theta):
    inv = jnp.power(theta, -jnp.arange(0, n_dims, 2, dtype=F32) / n_dims)
    ang = pos.astype(F32)[:, None] * inv[None, :]
    return jnp.cos(ang), jnp.sin(ang)


def _lane_tables(parts, s):
    cs, lo, hi = [], [], []
    zero = None
    for cos, sin in parts:
        zero = jnp.zeros_like(sin)
        cs += [cos, cos]
        lo += [-sin, zero]
        hi += [zero, sin]
    used = sum(a.shape[1] for a in cs)
    pad1 = jnp.ones((s, HEAD_DIM - used), F32)
    pad0 = jnp.zeros((s, HEAD_DIM - used), F32)
    build = lambda xs, pad: jnp.tile(jnp.concatenate(xs + [pad], axis=1), (1, LANES // HEAD_DIM))
    return build(cs, pad1), build(lo, pad0), build(hi, pad0)


def _ffn_weights(w_up, conv_w, conv_b, w_down):
    d = w_up.shape[0]
    n_c = D_FF // FF_CHUNK
    chunks = lambda w: w.reshape(d, n_c, FF_CHUNK).transpose(1, 0, 2).astype(BF16)
    wa, wg = chunks(w_up[:, :D_FF]), chunks(w_up[:, D_FF:])

    def conv_pack(cw, cb):
        rows = jnp.concatenate([cw, cb[None, :], jnp.zeros((4, D_FF), F32)], axis=0)
        return rows.reshape(8, n_c, FF_CHUNK).transpose(1, 0, 2)

    ca = conv_pack(conv_w[:, :D_FF], conv_b[:D_FF])
    cg = conv_pack(conv_w[:, D_FF:], conv_b[D_FF:])
    wd = w_down.astype(BF16)
    return wa, wg, ca, cg, wd


def _trunk(x, p, tm=512, tq=512):
    bsz, s, d = x.shape
    tm, tq, ck = min(tm, s), min(tq, s), min(KV_CHUNK, s)
    pos = jnp.arange(s)
    tabs_p = _lane_tables([_rope_tables(pos, ROPE_DIMS, ROPE_THETA)], s)
    tabs_a = _lane_tables([_rope_tables(pos // GRID_W, HEAD_DIM // 2, AXIAL_THETA),
                           _rope_tables(pos % GRID_W, HEAD_DIM // 2, AXIAL_THETA)], s)
    row = lambda v: v.reshape(1, -1)

    u, vn, q0, k0, v0 = _inproj0(x, p["w_in0"], row(p["a_ln_g"]), row(p["a_ln_b"]), tabs_p, tm)
    yb = _flash("diff", _q_transposed(q0, B_HEADS), k0, _v_transposed(v0, B_HEADS, ck),
                [row(p["b_lam_q1"]), row(p["b_lam_k1"]), row(p["b_lam_q2"]), row(p["b_lam_k2"]),
                 p["b_subln_g"].reshape(-1, 1)], tq)
    x = _mix0(x, u, vn, yb, p["ws_pair"], p["bs_x"], p["w_out0"], row(p["ln0a_g"]), row(p["ln0a_b"]), tm)
    x = _ffn(x, *p["ffn0"], row(p["ln0b_g"]), row(p["ln0b_b"]), tm)

    qc, kc, vc, *qkv_d = _inproj1(x, p["w_in1"], p["qg"], p["kg"], p["ones_blk"], tabs_p, tabs_a, tm)
    yc = _flash("gqa", _q_transposed(qc, C_Q_HEADS).reshape(bsz, 4, 2, HEAD_DIM, s), kc,
                _v_transposed(vc, C_KV_HEADS, ck), [], tq)
    pats = [_dilated(*qkv_d[3 * pi:3 * pi + 3], pi) for pi in range(len(D_PATTERNS))]
    x = _mix1(x, yc, pats, p["w_out1"], row(p["ln1a_g"]), row(p["ln1a_b"]), tm)
    x = _ffn(x, *p["ffn1"], row(p["ln1b_g"]), row(p["ln1b_b"]), tm)
    return x


def _prepare(w_in0, a_ln_g, a_ln_b, a_ws, a_bs, b_lam_q1, b_lam_k1, b_lam_q2, b_lam_k2, b_subln_g,
             w_out0, ln0a_g, ln0a_b, w_up0, conv_w0, conv_b0, w_down0, ln0b_g, ln0b_b,
             w_in1, c_qnorm_g, c_knorm_g, w_out1, ln1a_g, ln1a_b, w_up1, conv_w1, conv_b1, w_down1,
             ln1b_g, ln1b_b):
    grp = jnp.arange(LANES) // HEAD_DIM
    return dict(
        w_in0=w_in0.astype(BF16), a_ln_g=a_ln_g, a_ln_b=a_ln_b,
        ws_pair=a_ws.reshape(A_GROUPS // 2, 2, A_CHUNK, A_CHUNK).transpose(0, 2, 1, 3)
        .reshape(A_GROUPS // 2, A_CHUNK, 2 * A_CHUNK).astype(BF16),
        bs_x=jnp.repeat(a_bs.T, HEAD_DIM, axis=1),
        b_lam_q1=b_lam_q1, b_lam_k1=b_lam_k1, b_lam_q2=b_lam_q2, b_lam_k2=b_lam_k2, b_subln_g=b_subln_g,
        w_out0=w_out0.astype(BF16), ln0a_g=ln0a_g, ln0a_b=ln0a_b,
        ffn0=_ffn_weights(w_up0, conv_w0, conv_b0, w_down0), ln0b_g=ln0b_g, ln0b_b=ln0b_b,
        w_in1=w_in1.astype(BF16),
        qg=jnp.tile(c_qnorm_g, LANES // HEAD_DIM).reshape(1, LANES),
        kg=jnp.tile(c_knorm_g, LANES // HEAD_DIM).reshape(1, LANES),
        ones_blk=(grp[:, None] == grp[None, :]).astype(BF16),
        w_out1=w_out1.astype(BF16), ln1a_g=ln1a_g, ln1a_b=ln1a_b,
        ffn1=_ffn_weights(w_up1, conv_w1, conv_b1, w_down1), ln1b_g=ln1b_g, ln1b_b=ln1b_b,
    )


def kernel(x_prompt, x_sample, w_in0, a_ln_g, a_ln_b, a_ws, a_bs, b_lam_q1, b_lam_k1, b_lam_q2, b_lam_k2, b_subln_g, w_out0, ln0a_g, ln0a_b, w_up0, conv_w0, conv_b0, w_down0, ln0b_g, ln0b_b, w_in1, c_qnorm_g, c_knorm_g, w_out1, ln1a_g, ln1a_b, w_up1, conv_w1, conv_b1, w_down1, ln1b_g, ln1b_b):
    p = _prepare(w_in0, a_ln_g, a_ln_b, a_ws, a_bs, b_lam_q1, b_lam_k1, b_lam_q2, b_lam_k2, b_subln_g,
                 w_out0, ln0a_g, ln0a_b, w_up0, conv_w0, conv_b0, w_down0, ln0b_g, ln0b_b,
                 w_in1, c_qnorm_g, c_knorm_g, w_out1, ln1a_g, ln1a_b, w_up1, conv_w1, conv_b1, w_down1,
                 ln1b_g, ln1b_b)
    return (_trunk(x_prompt, p), _trunk(x_sample, p))
```

```python
import functools
import math

import jax
import jax.numpy as jnp
from jax import lax
from jax.experimental import pallas as pl
from jax.experimental.pallas import tpu as pltpu

F32 = jnp.float32
BF16 = jnp.bfloat16

D_MODEL = 1024
DEPTH = 2
HEAD_DIM = 64
GRID_W = 64
LN_EPS = 1e-5
RMS_EPS = 1e-6
ROPE_THETA = 500000.0
ROPE_DIMS = HEAD_DIM // 4
AXIAL_THETA = 10000.0
A_GROUPS = 8
A_WIDTH = A_GROUPS * HEAD_DIM
A_CHUNK = 128
B_HEADS = 4
B_V_DIM = 2 * HEAD_DIM
B_LAMBDA_INIT = 0.8 - 0.6 * math.exp(-0.3 * 0)
C_Q_HEADS = 8
C_KV_HEADS = 2
D_PATTERNS = ((128, 1), (512, 4), (2048, 16))
D_SLOTS = 4
D_WIDTH = D_SLOTS * HEAD_DIM
D_HALF_WINDOW = 64
D_FF = 2816
FF_CHUNK = 256
FF_DOWN_GROUP = 4
ALPHA = (2 * DEPTH) ** 0.25
QK_SCALE = HEAD_DIM ** -0.5 * math.log2(math.e)
NEG_BIG = -1e30
BOUND_SLACK = 1.0 + 2.0 ** -6
SAFE_DENOM = 2.0 ** -64
KV_CHUNK = 2048
SLOW_KV_STEP = 256

LANES = 128
ONES_ROWS = 16
VMEM_LIMIT = 56 * 1024 * 1024


def _params(*sem):
    return pltpu.CompilerParams(dimension_semantics=sem, vmem_limit_bytes=VMEM_LIMIT)


def _resident(shape):
    nd = len(shape)
    return pl.BlockSpec(shape, lambda *_: (0,) * nd, pipeline_mode=pl.Buffered(1))


def _layer_norm(x, g, b):
    mu = jnp.mean(x, -1, keepdims=True)
    xc = x - mu
    var = jnp.mean(xc * xc, -1, keepdims=True)
    return xc * lax.rsqrt(var + LN_EPS) * g + b


def _rope(x, c, s_lo, s_hi, shift):
    return (x * c + pltpu.roll(x, LANES - shift, 1) * s_lo + pltpu.roll(x, shift, 1) * s_hi)


def _dot(a, b):
    return jnp.dot(a, b, preferred_element_type=F32)


def _inproj0_kernel(x_ref, w_ref, lng_ref, lnb_ref, c_ref, s1_ref, s2_ref,
                    u_ref, vn_ref, q_ref, k_ref, vb_ref):
    xb = x_ref[...].astype(BF16)
    za = jax.nn.gelu(_dot(xb, w_ref[:, 0:2 * A_WIDTH]))
    u_ref[...] = za[:, :A_WIDTH]
    vn_ref[...] = _layer_norm(za[:, A_WIDTH:], lng_ref[...], lnb_ref[...]).astype(BF16)
    c, s1, s2 = c_ref[...], s1_ref[...], s2_ref[...]
    base = 2 * A_WIDTH
    hq = _dot(xb, w_ref[:, base:base + 512])
    hk = _dot(xb, w_ref[:, base + 512:base + 1024])
    for j in range(4):
        sl = slice(j * LANES, (j + 1) * LANES)
        q_ref[:, sl] = (_rope(hq[:, sl], c, s1, s2, ROPE_DIMS // 2) * QK_SCALE).astype(BF16)
        k_ref[:, sl] = _rope(hk[:, sl], c, s1, s2, ROPE_DIMS // 2).astype(BF16)
    vb_ref[...] = _dot(xb, w_ref[:, base + 1024:base + 1536]).astype(BF16)


def _inproj0(x, w, lng, lnb, tabs, tm):
    bsz, s, d = x.shape
    n_in = w.shape[1]
    tok = lambda width: pl.BlockSpec((None, tm, width), lambda b, i: (b, i, 0))
    tab = pl.BlockSpec((tm, LANES), lambda b, i: (i, 0))
    outs = [jax.ShapeDtypeStruct((bsz, s, 512), F32)] + [jax.ShapeDtypeStruct((bsz, s, 512), BF16)] * 4
    return pl.pallas_call(
        _inproj0_kernel,
        grid=(bsz, s // tm),
        in_specs=[tok(d), _resident((d, n_in)), _resident((1, 512)), _resident((1, 512)), tab, tab, tab],
        out_specs=[tok(512)] * 5,
        out_shape=outs,
        compiler_params=_params("parallel", "parallel"),
        name="inproj0",
    )(x, w, lng, lnb, *tabs)


def _group_rms(x, gain, ones_blk):
    x2 = x * x
    hi = x2.astype(BF16)
    lo = (x2 - hi.astype(F32)).astype(BF16)
    ss = _dot(hi, ones_blk) + _dot(lo, ones_blk)
    return x * lax.rsqrt(ss * (1.0 / HEAD_DIM) + RMS_EPS) * gain


def _inproj1_kernel(x_ref, w_ref, qg_ref, kg_ref, ones_ref, c_ref, s1_ref, s2_ref,
                    ca_ref, sa1_ref, sa2_ref,
                    qc_ref, kc_ref, vc_ref, *rest, tm):
    d_refs, scr_refs = rest[:9], rest[9:]

    def emit_classes(kind, vals):
        for pi, (_, dil) in enumerate(D_PATTERNS):
            out_ref = d_refs[3 * pi + kind]
            blk = vals[:, pi * D_WIDTH:(pi + 1) * D_WIDTH]
            if dil == 1:
                out_ref[0] = blk.astype(BF16)
                continue
            scr = scr_refs[3 * (pi - 1) + kind]
            for j in range(D_WIDTH // LANES):
                scr[j] = blk[:, j * LANES:(j + 1) * LANES]
            for c in range(dil):
                out_ref[c] = jnp.concatenate(
                    [scr[j, pl.ds(c, tm // dil, stride=dil), :] for j in range(D_WIDTH // LANES)],
                    axis=1).astype(BF16)

    xb = x_ref[...].astype(BF16)
    ones_blk = ones_ref[...]
    ca, sa1, sa2 = ca_ref[...], sa1_ref[...], sa2_ref[...]
    hq = _dot(xb, w_ref[:, 0:512])
    for j in range(4):
        sl = slice(j * LANES, (j + 1) * LANES)
        qn = _group_rms(hq[:, sl], qg_ref[...], ones_blk)
        qc_ref[:, sl] = (_rope(qn, ca, sa1, sa2, HEAD_DIM // 4) * QK_SCALE).astype(BF16)
    hkv = _dot(xb, w_ref[:, 512:768])
    kn = _group_rms(hkv[:, :LANES], kg_ref[...], ones_blk)
    kc_ref[...] = _rope(kn, ca, sa1, sa2, HEAD_DIM // 4).astype(BF16)
    vc_ref[...] = hkv[:, LANES:].astype(BF16)
    c, s1, s2 = c_ref[...], s1_ref[...], s2_ref[...]
    hqd = _dot(xb, w_ref[:, 768:1536])
    hkd = _dot(xb, w_ref[:, 1536:2304])
    blocks = range(3 * D_WIDTH // LANES)
    lane_block = lambda j: slice(j * LANES, (j + 1) * LANES)
    emit_classes(0, jnp.concatenate(
        [_rope(hqd[:, lane_block(j)], c, s1, s2, ROPE_DIMS // 2) * QK_SCALE for j in blocks], axis=1))
    emit_classes(1, jnp.concatenate(
        [_rope(hkd[:, lane_block(j)], c, s1, s2, ROPE_DIMS // 2) for j in blocks], axis=1))
    emit_classes(2, _dot(xb, w_ref[:, 2304:3072]))


def _inproj1(x, w, qg, kg, ones_blk, tabs_p, tabs_a, tm):
    bsz, s, d = x.shape
    n_in = w.shape[1]
    tok = lambda width: pl.BlockSpec((None, tm, width), lambda b, i: (b, i, 0))
    tab = pl.BlockSpec((tm, LANES), lambda b, i: (i, 0))
    widths = (512, 128, 128)
    out_specs = [tok(wd) for wd in widths]
    out_shape = [jax.ShapeDtypeStruct((bsz, s, wd), BF16) for wd in widths]
    scratch = []
    for _, dil in D_PATTERNS:
        out_specs += [pl.BlockSpec((None, dil, tm // dil, D_WIDTH), lambda b, i: (b, 0, i, 0))] * 3
        out_shape += [jax.ShapeDtypeStruct((bsz, dil, s // dil, D_WIDTH), BF16)] * 3
        if dil > 1:
            scratch += [pltpu.VMEM((D_WIDTH // LANES, tm, LANES), F32)] * 3
    return pl.pallas_call(
        functools.partial(_inproj1_kernel, tm=tm),
        grid=(bsz, s // tm),
        in_specs=[tok(d), _resident((d, n_in)), _resident((1, LANES)), _resident((1, LANES)),
                  _resident((LANES, LANES))] + [tab] * 6,
        out_specs=out_specs,
        out_shape=out_shape,
        scratch_shapes=scratch,
        compiler_params=_params("parallel", "parallel"),
        name="inproj1",
    )(x, w, qg, kg, ones_blk, *tabs_p, *tabs_a)


def _flash_kernel(*refs, mode, dv, tq, ck):
    if mode == "diff":
        (q_ref, k_ref, vt_ref, lq1_ref, lk1_ref, lq2_ref, lk2_ref, g_ref,
         o_ref, qp_ref, mb_ref, acc_ref, kmax_ref) = refs
    else:
        q_ref, k_ref, vt_ref, o_ref, qp_ref, mb_ref, acc_ref, kmax_ref = refs
    n_chunks = vt_ref.shape[0]
    kv_head = pl.program_id(1) // 2
    lane_grp = lax.broadcasted_iota(jnp.int32, (1, LANES), 1) // HEAD_DIM

    def k_chunk(c):
        return k_ref[pl.ds(pl.multiple_of(c * ck, ck), ck), :]

    @pl.when(pl.program_id(2) == 0)
    def _():
        r = lax.broadcasted_iota(jnp.int32, (LANES, LANES), 0) // HEAD_DIM
        c_ = lax.broadcasted_iota(jnp.int32, (LANES, LANES), 1) // HEAD_DIM
        ones_blk = (r == c_).astype(BF16)

        def body(c, mx):
            kf = k_chunk(c).astype(F32)
            return jnp.maximum(mx, jnp.max(_dot((kf * kf).astype(BF16), ones_blk), axis=0, keepdims=True))

        kmax_ref[...] = lax.fori_loop(0, n_chunks, body, jnp.zeros((1, LANES), F32))

    row = lax.broadcasted_iota(jnp.int32, (2 * HEAD_DIM, tq), 0)
    for m in range(2):
        if mode == "diff":
            q = q_ref[...]
            keep = (row >= HEAD_DIM) == (m == 1)
            grp = m
        else:
            q = jnp.concatenate([q_ref[m], q_ref[m]], axis=0)
            keep = (row >= HEAD_DIM) == (kv_head == 1)
            grp = kv_head
        qp = jnp.where(keep, q, jnp.zeros_like(q))
        qp_ref[m] = qp
        qf = qp.astype(F32)
        qn2 = jnp.sum(qf * qf, axis=0, keepdims=True)
        kmax2 = jnp.max(jnp.where(lane_grp == grp, kmax_ref[...], 0.0), axis=1, keepdims=True)
        mb_ref[m] = jnp.sqrt(qn2 * kmax2) * BOUND_SLACK
    acc_ref[...] = jnp.zeros(acc_ref.shape, F32)

    def fast_body(c, carry):
        k = k_chunk(c)
        vt = vt_ref[c]
        for m in range(2):
            p = jnp.exp2(_dot(k, qp_ref[m]) - mb_ref[m]).astype(BF16)
            acc_ref[m] += _dot(vt, p)
        return carry

    lax.fori_loop(0, n_chunks, fast_body, 0)

    denom_min = jnp.min(jnp.minimum(acc_ref[0, dv:dv + 1, :], acc_ref[1, dv:dv + 1, :]))

    @pl.when(jnp.logical_not(denom_min >= SAFE_DENOM))
    def _():
        mb_ref[...] = jnp.full(mb_ref.shape, NEG_BIG, F32)
        acc_ref[...] = jnp.zeros(acc_ref.shape, F32)

        sub = min(ck, SLOW_KV_STEP)

        def slow_body(j, carry):
            start = pl.multiple_of(j * sub, sub)
            k = k_ref[pl.ds(start, sub), :]
            vt = vt_ref[start // ck, :, pl.ds(pl.multiple_of(start % ck, sub), sub)]
            for m in range(2):
                s = _dot(k, qp_ref[m])
                m_old = mb_ref[m]
                m_new = jnp.maximum(m_old, jnp.max(s, axis=0, keepdims=True))
                p = jnp.exp2(s - m_new).astype(BF16)
                acc_ref[m] = jnp.exp2(m_old - m_new) * acc_ref[m] + _dot(vt, p)
                mb_ref[m] = m_new
            return carry

        lax.fori_loop(0, n_chunks * (ck // sub), slow_body, 0)

    a0, a1 = acc_ref[0], acc_ref[1]
    o0 = a0[:dv] * (1.0 / a0[dv:dv + 1])
    o1 = a1[:dv] * (1.0 / a1[dv:dv + 1])
    if mode == "diff":
        lam = (jnp.exp(jnp.sum(lq1_ref[...] * lk1_ref[...], keepdims=True))
               - jnp.exp(jnp.sum(lq2_ref[...] * lk2_ref[...], keepdims=True)) + B_LAMBDA_INIT)
        o = o0 - lam * o1
        ms = jnp.mean(o * o, axis=0, keepdims=True)
        o = o * lax.rsqrt(ms + RMS_EPS) * g_ref[...] * (1.0 - B_LAMBDA_INIT)
    else:
        o = jnp.concatenate([o0, o1], axis=0)
    o_ref[...] = o.T.astype(o_ref.dtype)


def _flash(mode, qt, k, vt, extra, tq):
    bsz, s = k.shape[0], k.shape[1]
    n_chunks, dve, ck = vt.shape[2:]
    dv = dve - ONES_ROWS
    if mode == "diff":
        q_spec = pl.BlockSpec((None, None, 2 * HEAD_DIM, tq), lambda b, h, qi: (b, h, 0, qi))
        k_spec = pl.BlockSpec((None, s, LANES), lambda b, h, qi: (b, 0, h))
        vt_spec = pl.BlockSpec((None, None, n_chunks, dve, ck), lambda b, h, qi: (b, h, 0, 0, 0))
        extra_specs = [_resident((1, HEAD_DIM))] * 4 + [_resident((B_V_DIM, 1))]
    else:
        q_spec = pl.BlockSpec((None, None, 2, HEAD_DIM, tq), lambda b, h, qi: (b, h, 0, 0, qi))
        k_spec = pl.BlockSpec((None, s, LANES), lambda b, h, qi: (b, 0, 0))
        vt_spec = pl.BlockSpec((None, None, n_chunks, dve, ck), lambda b, h, qi: (b, h // 2, 0, 0, 0))
        extra_specs = []
    return pl.pallas_call(
        functools.partial(_flash_kernel, mode=mode, dv=dv, tq=tq, ck=ck),
        grid=(bsz, 4, s // tq),
        in_specs=[q_spec, k_spec, vt_spec] + extra_specs,
        out_specs=pl.BlockSpec((None, tq, LANES), lambda b, h, qi: (b, qi, h)),
        out_shape=jax.ShapeDtypeStruct((bsz, s, 512), BF16),
        scratch_shapes=[pltpu.VMEM((2, 2 * HEAD_DIM, tq), BF16),
                        pltpu.VMEM((2, 1, tq), F32),
                        pltpu.VMEM((2, dve, tq), F32),
                        pltpu.VMEM((1, LANES), F32)],
        compiler_params=_params("parallel", "parallel", "arbitrary"),
        name="flash_" + mode,
    )(qt, k, vt, *extra)


def _q_transposed(x, n_heads):
    bsz, s, width = x.shape
    return x.reshape(bsz, s, n_heads, width // n_heads).transpose(0, 2, 3, 1)


def _v_transposed(x, n_heads, ck):
    bsz, s, width = x.shape
    xt = x.reshape(bsz, s // ck, ck, n_heads, width // n_heads).transpose(0, 3, 1, 4, 2)
    return jnp.concatenate([xt, jnp.ones((bsz, n_heads, s // ck, ONES_ROWS, ck), x.dtype)], axis=3)


def _dilated_kernel(q_ref, kp_ref, km_ref, kn_ref, vp_ref, vm_ref, vn_ref,
                    o_ref, m_ref, l_ref, *, tq, sub_len):
    i0 = pl.program_id(2) * tq
    q = q_ref[...]
    kw = jnp.concatenate([kp_ref[...], km_ref[...], kn_ref[...]], axis=0)
    vw = jnp.concatenate([vp_ref[...], vm_ref[...], vn_ref[...]], axis=0)
    nk = tq + 2 * LANES
    t = lax.broadcasted_iota(jnp.int32, (tq, nk), 0)
    w = lax.broadcasted_iota(jnp.int32, (tq, nk), 1)
    off = w - LANES - t
    j = i0 - LANES + w
    valid = (off >= -D_HALF_WINDOW) & (off <= D_HALF_WINDOW) & (j >= 0) & (j < sub_len)
    lane = lax.broadcasted_iota(jnp.int32, (1, D_WIDTH), 1)
    o_acc = jnp.zeros((tq, D_WIDTH), F32)
    m_acc = jnp.zeros((tq, D_WIDTH), F32)
    l_acc = jnp.zeros((tq, D_WIDTH), F32)
    for h in range(D_SLOTS):
        in_head = (lane >= h * HEAD_DIM) & (lane < (h + 1) * HEAD_DIM)
        qh = jnp.where(in_head, q, jnp.zeros_like(q))
        s = lax.dot_general(qh, kw, (((1,), (1,)), ((), ())), preferred_element_type=F32)
        s = jnp.where(valid, s, NEG_BIG)
        mh = jnp.max(s, axis=1, keepdims=True)
        p = jnp.exp2(s - mh)
        lh = jnp.sum(p, axis=1, keepdims=True)
        oh = _dot(p.astype(BF16), vw)
        sel = in_head.astype(F32)
        o_acc = o_acc + oh * sel
        m_acc = m_acc + mh * sel
        l_acc = l_acc + lh * sel
    o_ref[...] = o_acc
    m_ref[...] = m_acc
    l_ref[...] = l_acc


def _dilated(qd, kd, vd, pat):
    bsz, dilation, sub_len, _ = qd.shape
    tq = min(256, sub_len)
    blk128 = sub_len // LANES
    per_tile = tq // LANES
    main = pl.BlockSpec((None, None, tq, D_WIDTH), lambda b, c, i: (b, c, i, 0))
    prev = pl.BlockSpec((None, None, LANES, D_WIDTH),
                        lambda b, c, i: (b, c, jnp.maximum(i * per_tile - 1, 0), 0))
    nxt = pl.BlockSpec((None, None, LANES, D_WIDTH),
                       lambda b, c, i: (b, c, jnp.minimum((i + 1) * per_tile, blk128 - 1), 0))
    out = jax.ShapeDtypeStruct((bsz, dilation, sub_len, D_WIDTH), F32)
    return pl.pallas_call(
        functools.partial(_dilated_kernel, tq=tq, sub_len=sub_len),
        grid=(bsz, dilation, sub_len // tq),
        in_specs=[main, prev, main, nxt, prev, main, nxt],
        out_specs=[main] * 3,
        out_shape=[out] * 3,
        compiler_params=_params("parallel", "parallel", "parallel"),
        name="dilated_p%d" % pat,
    )(qd, kd, kd, kd, vd, vd, vd)


def _mix0_kernel(x_ref, u_ref, vn_ref, yb_ref, ws_ref, bs_ref, w_ref, g_ref, b_ref, o_ref, *, tm):
    lane = lax.broadcasted_iota(jnp.int32, (A_CHUNK, LANES), 1)
    left = lane < HEAD_DIM
    rows = []
    for n in range(tm // A_CHUNK):
        r = slice(n * A_CHUNK, (n + 1) * A_CHUNK)
        cols = []
        for j in range(A_WIDTH // LANES):
            v = vn_ref[r, j * LANES:(j + 1) * LANES]
            zero = jnp.zeros_like(v)
            stacked = jnp.concatenate([jnp.where(left, v, zero), jnp.where(left, zero, v)], axis=0)
            cols.append(_dot(ws_ref[j], stacked))
        rows.append(jnp.concatenate(cols, axis=1) + bs_ref[...])
    mixed = jnp.concatenate(rows, axis=0)
    ya = (u_ref[...] * mixed).astype(BF16)
    y = _dot(ya, w_ref[0:A_WIDTH, :]) + _dot(yb_ref[...], w_ref[A_WIDTH:, :])
    o_ref[...] = _layer_norm(ALPHA * x_ref[...] + y, g_ref[...], b_ref[...])


def _mix0(x, u, vn, yb, ws_pair, bs_x, w_out, g, b, tm):
    bsz, s, d = x.shape
    tok = lambda width: pl.BlockSpec((None, tm, width), lambda bb, i: (bb, i, 0))
    return pl.pallas_call(
        functools.partial(_mix0_kernel, tm=tm),
        grid=(bsz, s // tm),
        in_specs=[tok(d), tok(512), tok(512), tok(512), _resident(ws_pair.shape), _resident(bs_x.shape),
                  _resident(w_out.shape), _resident((1, d)), _resident((1, d))],
        out_specs=tok(d),
        out_shape=jax.ShapeDtypeStruct((bsz, s, d), F32),
        compiler_params=_params("parallel", "parallel"),
        name="mix0",
    )(x, u, vn, yb, ws_pair, bs_x, w_out, g, b)


def _mix1_kernel(x_ref, yc_ref, *refs, tm):
    pat_refs, (w_ref, g_ref, b_ref, o_ref), scr_refs = refs[:9], refs[9:13], refs[13:]

    def token_major(idx):
        dil = D_PATTERNS[idx // 3][1]
        ref = pat_refs[idx]
        if dil == 1:
            return ref[0]
        scr = scr_refs[idx - 3]
        for c in range(dil):
            for j in range(D_WIDTH // LANES):
                scr[j, pl.ds(c, tm // dil, stride=dil), :] = ref[c, :, j * LANES:(j + 1) * LANES]
        return jnp.concatenate([scr[j] for j in range(D_WIDTH // LANES)], axis=1)

    vals = [token_major(i) for i in range(9)]
    os_, ms_, ls_ = vals[0::3], vals[1::3], vals[2::3]
    m_all = jnp.maximum(jnp.maximum(ms_[0], ms_[1]), ms_[2])
    num = jnp.zeros(m_all.shape, F32)
    den = jnp.zeros(m_all.shape, F32)
    for p in range(3):
        wgt = jnp.exp2(ms_[p] - m_all)
        num = num + wgt * os_[p]
        den = den + wgt * ls_[p]
    yd = (num * (1.0 / den)).astype(BF16)
    y = _dot(yc_ref[...], w_ref[0:512, :]) + _dot(yd, w_ref[512:, :])
    o_ref[...] = _layer_norm(ALPHA * x_ref[...] + y, g_ref[...], b_ref[...])


def _mix1(x, yc, pats, w_out, g, b, tm):
    bsz, s, d = x.shape
    tok = lambda width: pl.BlockSpec((None, tm, width), lambda bb, i: (bb, i, 0))
    flat = [a for oml in pats for a in oml]
    pat_specs = [pl.BlockSpec((None, a.shape[1], tm // a.shape[1], D_WIDTH), lambda bb, i: (bb, 0, i, 0))
                 for a in flat]
    return pl.pallas_call(
        functools.partial(_mix1_kernel, tm=tm),
        grid=(bsz, s // tm),
        in_specs=[tok(d), tok(512)] + pat_specs
                 + [_resident(w_out.shape), _resident((1, d)), _resident((1, d))],
        out_specs=tok(d),
        out_shape=jax.ShapeDtypeStruct((bsz, s, d), F32),
        scratch_shapes=[pltpu.VMEM((D_WIDTH // LANES, tm, LANES), F32)] * 6,
        compiler_params=_params("parallel", "parallel"),
        name="mix1",
    )(x, yc, *flat, w_out, g, b)


def _ffn_kernel(xp_ref, xm_ref, xn_ref, wa_ref, wg_ref, ca_ref, cg_ref, wd_ref, g_ref, b_ref,
                o_ref, xe_ref, *, tm, tiles_per_seq):
    i = pl.program_id(0) % tiles_per_seq
    halo = 8
    prev = jnp.where(i > 0, xp_ref[...], 0.0)
    nxt = jnp.where(i < tiles_per_seq - 1, xn_ref[...], 0.0)
    xe_ref[...] = jnp.concatenate([prev, xm_ref[...], nxt], axis=0).astype(BF16)
    ext = tm + 2 * halo

    def conv(h, cw):
        mid = slice(halo, halo + tm)
        return (cw[0:1] * pltpu.roll(h, 1, 0)[mid] + cw[1:2] * h[mid]
                + cw[2:3] * pltpu.roll(h, ext - 1, 0)[mid] + cw[3:4])

    n_c = wa_ref.shape[0]
    y = None
    acts = []
    for c in range(n_c):
        xe = xe_ref[...]
        a = conv(_dot(xe, wa_ref[c]), ca_ref[c])
        gt = conv(_dot(xe, wg_ref[c]), cg_ref[c])
        acts.append((jax.nn.gelu(gt) * a).astype(BF16))
        if len(acts) == FF_DOWN_GROUP or c == n_c - 1:
            c0 = c + 1 - len(acts)
            part = _dot(jnp.concatenate(acts, axis=1), wd_ref[c0 * FF_CHUNK:(c + 1) * FF_CHUNK, :])
            y = part if y is None else y + part
            acts = []
    o_ref[...] = _layer_norm(ALPHA * xm_ref[...] + y, g_ref[...], b_ref[...])


def _ffn(x, wa, wg, ca, cg, wd, g, b, tm):
    bsz, s, d = x.shape
    n_tok = bsz * s
    xf = x.reshape(n_tok, d)
    tiles_per_seq = s // tm
    r8 = tm // 8
    n8 = n_tok // 8
    out = pl.pallas_call(
        functools.partial(_ffn_kernel, tm=tm, tiles_per_seq=tiles_per_seq),
        grid=(n_tok // tm,),
        in_specs=[pl.BlockSpec((8, d), lambda i: (jnp.maximum(i * r8 - 1, 0), 0)),
                  pl.BlockSpec((tm, d), lambda i: (i, 0)),
                  pl.BlockSpec((8, d), lambda i: (jnp.minimum((i + 1) * r8, n8 - 1), 0)),
                  _resident(wa.shape), _resident(wg.shape), _resident(ca.shape), _resident(cg.shape),
                  _resident(wd.shape), _resident((1, d)), _resident((1, d))],
        out_specs=pl.BlockSpec((tm, d), lambda i: (i, 0)),
        out_shape=jax.ShapeDtypeStruct((n_tok, d), F32),
        scratch_shapes=[pltpu.VMEM((tm + 16, d), BF16)],
        compiler_params=_params("parallel"),
        name="conv_ffn",
    )(xf, xf, xf, wa, wg, ca, cg, wd, g, b)
    return out.reshape(bsz, s, d)


def _rope_tables(pos, n_dims, theta):
    inv = jnp.power(theta, -jnp.arange(0, n_dims, 2, dtype=F32) / n_dims)
    ang = pos.astype(F32)[:, None] * inv[None, :]
    return jnp.cos(ang), jnp.sin(ang)


def _lane_tables(parts, s):
    cs, lo, hi = [], [], []
    zero = None
    for cos, sin in parts:
        zero = jnp.zeros_like(sin)
        cs += [cos, cos]
        lo += [-sin, zero]
        hi += [zero, sin]
    used = sum(a.shape[1] for a in cs)
    pad1 = jnp.ones((s, HEAD_DIM - used), F32)
    pad0 = jnp.zeros((s, HEAD_DIM - used), F32)
    build = lambda xs, pad: jnp.tile(jnp.concatenate(xs + [pad], axis=1), (1, LANES // HEAD_DIM))
    return build(cs, pad1), build(lo, pad0), build(hi, pad0)


def _ffn_weights(w_up, conv_w, conv_b, w_down):
    d = w_up.shape[0]
    n_c = D_FF // FF_CHUNK
    chunks = lambda w: w.reshape(d, n_c, FF_CHUNK).transpose(1, 0, 2).astype(BF16)
    wa, wg = chunks(w_up[:, :D_FF]), chunks(w_up[:, D_FF:])

    def conv_pack(cw, cb):
        rows = jnp.concatenate([cw, cb[None, :], jnp.zeros((4, D_FF), F32)], axis=0)
        return rows.reshape(8, n_c, FF_CHUNK).transpose(1, 0, 2)

    ca = conv_pack(conv_w[:, :D_FF], conv_b[:D_FF])
    cg = conv_pack(conv_w[:, D_FF:], conv_b[D_FF:])
    wd = w_down.astype(BF16)
    return wa, wg, ca, cg, wd


def _trunk(x, p, tm=512, tq=1024):
    bsz, s, d = x.shape
    tm, tq, ck = min(tm, s), min(tq, s), min(KV_CHUNK, s)
    pos = jnp.arange(s)
    tabs_p = _lane_tables([_rope_tables(pos, ROPE_DIMS, ROPE_THETA)], s)
    tabs_a = _lane_tables([_rope_tables(pos // GRID_W, HEAD_DIM // 2, AXIAL_THETA),
                           _rope_tables(pos % GRID_W, HEAD_DIM // 2, AXIAL_THETA)], s)
    row = lambda v: v.reshape(1, -1)

    u, vn, q0, k0, v0 = _inproj0(x, p["w_in0"], row(p["a_ln_g"]), row(p["a_ln_b"]), tabs_p, tm)
    yb = _flash("diff", _q_transposed(q0, B_HEADS), k0, _v_transposed(v0, B_HEADS, ck),
                [row(p["b_lam_q1"]), row(p["b_lam_k1"]), row(p["b_lam_q2"]), row(p["b_lam_k2"]),
                 p["b_subln_g"].reshape(-1, 1)], tq)
    x = _mix0(x, u, vn, yb, p["ws_pair"], p["bs_x"], p["w_out0"], row(p["ln0a_g"]), row(p["ln0a_b"]), tm)
    x = _ffn(x, *p["ffn0"], row(p["ln0b_g"]), row(p["ln0b_b"]), tm)

    qc, kc, vc, *qkv_d = _inproj1(x, p["w_in1"], p["qg"], p["kg"], p["ones_blk"], tabs_p, tabs_a, tm)
    yc = _flash("gqa", _q_transposed(qc, C_Q_HEADS).reshape(bsz, 4, 2, HEAD_DIM, s), kc,
                _v_transposed(vc, C_KV_HEADS, ck), [], tq)
    pats = [_dilated(*qkv_d[3 * pi:3 * pi + 3], pi) for pi in range(len(D_PATTERNS))]
    x = _mix1(x, yc, pats, p["w_out1"], row(p["ln1a_g"]), row(p["ln1a_b"]), tm)
    x = _ffn(x, *p["ffn1"], row(p["ln1b_g"]), row(p["ln1b_b"]), tm)
    return x


def _prepare(w_in0, a_ln_g, a_ln_b, a_ws, a_bs, b_lam_q1, b_lam_k1, b_lam_q2, b_lam_k2, b_subln_g,
             w_out0, ln0a_g, ln0a_b, w_up0, conv_w0, conv_b0, w_down0, ln0b_g, ln0b_b,
             w_in1, c_qnorm_g, c_knorm_g, w_out1, ln1a_g, ln1a_b, w_up1, conv_w1, conv_b1, w_down1,
             ln1b_g, ln1b_b):
    grp = jnp.arange(LANES) // HEAD_DIM
    return dict(
        w_in0=w_in0.astype(BF16), a_ln_g=a_ln_g, a_ln_b=a_ln_b,
        ws_pair=a_ws.reshape(A_GROUPS // 2, 2, A_CHUNK, A_CHUNK).transpose(0, 2, 1, 3)
        .reshape(A_GROUPS // 2, A_CHUNK, 2 * A_CHUNK).astype(BF16),
        bs_x=jnp.repeat(a_bs.T, HEAD_DIM, axis=1),
        b_lam_q1=b_lam_q1, b_lam_k1=b_lam_k1, b_lam_q2=b_lam_q2, b_lam_k2=b_lam_k2, b_subln_g=b_subln_g,
        w_out0=w_out0.astype(BF16), ln0a_g=ln0a_g, ln0a_b=ln0a_b,
        ffn0=_ffn_weights(w_up0, conv_w0, conv_b0, w_down0), ln0b_g=ln0b_g, ln0b_b=ln0b_b,
        w_in1=w_in1.astype(BF16),
        qg=jnp.tile(c_qnorm_g, LANES // HEAD_DIM).reshape(1, LANES),
        kg=jnp.tile(c_knorm_g, LANES // HEAD_DIM).reshape(1, LANES),
        ones_blk=(grp[:, None] == grp[None, :]).astype(BF16),
        w_out1=w_out1.astype(BF16), ln1a_g=ln1a_g, ln1a_b=ln1a_b,
        ffn1=_ffn_weights(w_up1, conv_w1, conv_b1, w_down1), ln1b_g=ln1b_g, ln1b_b=ln1b_b,
    )


def kernel(x_prompt, x_sample, w_in0, a_ln_g, a_ln_b, a_ws, a_bs, b_lam_q1, b_lam_k1, b_lam_q2, b_lam_k2, b_subln_g, w_out0, ln0a_g, ln0a_b, w_up0, conv_w0, conv_b0, w_down0, ln0b_g, ln0b_b, w_in1, c_qnorm_g, c_knorm_g, w_out1, ln1a_g, ln1a_b, w_up1, conv_w1, conv_b1, w_down1, ln1b_g, ln1b_b):
    p = _prepare(w_in0, a_ln_g, a_ln_b, a_ws, a_bs, b_lam_q1, b_lam_k1, b_lam_q2, b_lam_k2, b_subln_g,
                 w_out0, ln0a_g, ln0a_b, w_up0, conv_w0, conv_b0, w_down0, ln0b_g, ln0b_b,
                 w_in1, c_qnorm_g, c_knorm_g, w_out1, ln1a_g, ln1a_b, w_up1, conv_w1, conv_b1, w_down1,
                 ln1b_g, ln1b_b)
    return (_trunk(x_prompt, p), _trunk(x_sample, p))
```

```python
import functools
import math

import jax
import jax.numpy as jnp
from jax import lax
from jax.experimental import pallas as pl
from jax.experimental.pallas import tpu as pltpu

F32 = jnp.float32
BF16 = jnp.bfloat16
F8 = jnp.float8_e4m3fn
F8_TARGET = 128.0
TINY_NORM_SQ = 1e-30
MAX_SCALE_EXP = 40.0

D_MODEL = 1024
DEPTH = 2
HEAD_DIM = 64
GRID_W = 64
LN_EPS = 1e-5
RMS_EPS = 1e-6
ROPE_THETA = 500000.0
ROPE_DIMS = HEAD_DIM // 4
AXIAL_THETA = 10000.0
A_GROUPS = 8
A_WIDTH = A_GROUPS * HEAD_DIM
A_CHUNK = 128
B_HEADS = 4
B_V_DIM = 2 * HEAD_DIM
B_LAMBDA_INIT = 0.8 - 0.6 * math.exp(-0.3 * 0)
C_Q_HEADS = 8
C_KV_HEADS = 2
D_PATTERNS = ((128, 1), (512, 4), (2048, 16))
D_SLOTS = 4
D_WIDTH = D_SLOTS * HEAD_DIM
D_HALF_WINDOW = 64
D_FF = 2816
FF_CHUNK = 256
FF_DOWN_GROUP = 4
ALPHA = (2 * DEPTH) ** 0.25
QK_SCALE = HEAD_DIM ** -0.5 * math.log2(math.e)
NEG_BIG = -1e30
BOUND_SLACK = 1.0 + 2.0 ** -6
SAFE_DENOM = 2.0 ** -64
KV_CHUNK = 1024
SLOW_KV_STEP = 256

LANES = 128
MXU_COLS = 256
ONES_ROWS = 16
VMEM_LIMIT = 56 * 1024 * 1024


def _params(*sem):
    return pltpu.CompilerParams(dimension_semantics=sem, vmem_limit_bytes=VMEM_LIMIT)


def _resident(shape):
    nd = len(shape)
    return pl.BlockSpec(shape, lambda *_: (0,) * nd, pipeline_mode=pl.Buffered(1))


def _layer_norm(x, g, b):
    mu = jnp.mean(x, -1, keepdims=True)
    xc = x - mu
    var = jnp.mean(xc * xc, -1, keepdims=True)
    return xc * lax.rsqrt(var + LN_EPS) * g + b


def _rope(x, c, s_lo, s_hi, shift):
    return (x * c + pltpu.roll(x, LANES - shift, 1) * s_lo + pltpu.roll(x, shift, 1) * s_hi)


def _dot(a, b):
    return jnp.dot(a, b, preferred_element_type=F32)


def _pow2_scale(max_sq):
    norm = jnp.sqrt(jnp.maximum(max_sq, TINY_NORM_SQ))
    return jnp.exp2(jnp.minimum(jnp.floor(jnp.log2(F8_TARGET / norm)), MAX_SCALE_EXP))


def _inproj0_kernel(x_ref, w_ref, lng_ref, lnb_ref, c_ref, s1_ref, s2_ref,
                    u_ref, vn_ref, q_ref, k_ref, vb_ref):
    xb = x_ref[...].astype(BF16)
    za = jax.nn.gelu(_dot(xb, w_ref[:, 0:2 * A_WIDTH]))
    u_ref[...] = za[:, :A_WIDTH]
    vn_ref[...] = _layer_norm(za[:, A_WIDTH:], lng_ref[...], lnb_ref[...]).astype(BF16)
    c, s1, s2 = c_ref[...], s1_ref[...], s2_ref[...]
    base = 2 * A_WIDTH
    hq = _dot(xb, w_ref[:, base:base + 512])
    hk = _dot(xb, w_ref[:, base + 512:base + 1024])
    for j in range(4):
        sl = slice(j * LANES, (j + 1) * LANES)
        q_ref[:, sl] = (_rope(hq[:, sl], c, s1, s2, ROPE_DIMS // 2) * QK_SCALE).astype(BF16)
        k_ref[:, sl] = _rope(hk[:, sl], c, s1, s2, ROPE_DIMS // 2).astype(BF16)
    vb_ref[...] = _dot(xb, w_ref[:, base + 1024:base + 1536]).astype(BF16)


def _inproj0(x, w, lng, lnb, tabs, tm):
    bsz, s, d = x.shape
    n_in = w.shape[1]
    tok = lambda width: pl.BlockSpec((None, tm, width), lambda b, i: (b, i, 0))
    tab = pl.BlockSpec((tm, LANES), lambda b, i: (i, 0))
    outs = [jax.ShapeDtypeStruct((bsz, s, 512), F32)] + [jax.ShapeDtypeStruct((bsz, s, 512), BF16)] * 4
    return pl.pallas_call(
        _inproj0_kernel,
        grid=(bsz, s // tm),
        in_specs=[tok(d), _resident((d, n_in)), _resident((1, 512)), _resident((1, 512)), tab, tab, tab],
        out_specs=[tok(512)] * 5,
        out_shape=outs,
        compiler_params=_params("parallel", "parallel"),
        name="inproj0",
    )(x, w, lng, lnb, *tabs)


def _group_rms(x, gain, ones_blk):
    x2 = x * x
    hi = x2.astype(BF16)
    lo = (x2 - hi.astype(F32)).astype(BF16)
    ss = _dot(hi, ones_blk) + _dot(lo, ones_blk)
    return x * lax.rsqrt(ss * (1.0 / HEAD_DIM) + RMS_EPS) * gain


def _inproj1_kernel(x_ref, w_ref, qg_ref, kg_ref, ones_ref, c_ref, s1_ref, s2_ref,
                    ca_ref, sa1_ref, sa2_ref,
                    qc_ref, kc_ref, vc_ref, *rest, tm):
    d_refs, scr_refs = rest[:9], rest[9:]

    def emit_classes(kind, vals):
        for pi, (_, dil) in enumerate(D_PATTERNS):
            out_ref = d_refs[3 * pi + kind]
            blk = vals[:, pi * D_WIDTH:(pi + 1) * D_WIDTH]
            if dil == 1:
                out_ref[0] = blk.astype(BF16)
                continue
            scr = scr_refs[3 * (pi - 1) + kind]
            for j in range(D_WIDTH // LANES):
                scr[j] = blk[:, j * LANES:(j + 1) * LANES]
            for c in range(dil):
                out_ref[c] = jnp.concatenate(
                    [scr[j, pl.ds(c, tm // dil, stride=dil), :] for j in range(D_WIDTH // LANES)],
                    axis=1).astype(BF16)

    xb = x_ref[...].astype(BF16)
    ones_blk = ones_ref[...]
    ca, sa1, sa2 = ca_ref[...], sa1_ref[...], sa2_ref[...]
    hq = _dot(xb, w_ref[:, 0:512])
    for j in range(4):
        sl = slice(j * LANES, (j + 1) * LANES)
        qn = _group_rms(hq[:, sl], qg_ref[...], ones_blk)
        qc_ref[:, sl] = (_rope(qn, ca, sa1, sa2, HEAD_DIM // 4) * QK_SCALE).astype(BF16)
    hkv = _dot(xb, w_ref[:, 512:768])
    kn = _group_rms(hkv[:, :LANES], kg_ref[...], ones_blk)
    kc_ref[...] = _rope(kn, ca, sa1, sa2, HEAD_DIM // 4).astype(BF16)
    vc_ref[...] = hkv[:, LANES:].astype(BF16)
    c, s1, s2 = c_ref[...], s1_ref[...], s2_ref[...]
    hqd = _dot(xb, w_ref[:, 768:1536])
    hkd = _dot(xb, w_ref[:, 1536:2304])
    blocks = range(3 * D_WIDTH // LANES)
    lane_block = lambda j: slice(j * LANES, (j + 1) * LANES)
    emit_classes(0, jnp.concatenate(
        [_rope(hqd[:, lane_block(j)], c, s1, s2, ROPE_DIMS // 2) * QK_SCALE for j in blocks], axis=1))
    emit_classes(1, jnp.concatenate(
        [_rope(hkd[:, lane_block(j)], c, s1, s2, ROPE_DIMS // 2) for j in blocks], axis=1))
    emit_classes(2, _dot(xb, w_ref[:, 2304:3072]))


def _inproj1(x, w, qg, kg, ones_blk, tabs_p, tabs_a, tm):
    bsz, s, d = x.shape
    n_in = w.shape[1]
    tok = lambda width: pl.BlockSpec((None, tm, width), lambda b, i: (b, i, 0))
    tab = pl.BlockSpec((tm, LANES), lambda b, i: (i, 0))
    widths = (512, 128, 128)
    out_specs = [tok(wd) for wd in widths]
    out_shape = [jax.ShapeDtypeStruct((bsz, s, wd), BF16) for wd in widths]
    scratch = []
    for _, dil in D_PATTERNS:
        out_specs += [pl.BlockSpec((None, dil, tm // dil, D_WIDTH), lambda b, i: (b, 0, i, 0))] * 3
        out_shape += [jax.ShapeDtypeStruct((bsz, dil, s // dil, D_WIDTH), BF16)] * 3
        if dil > 1:
            scratch += [pltpu.VMEM((D_WIDTH // LANES, tm, LANES), F32)] * 3
    return pl.pallas_call(
        functools.partial(_inproj1_kernel, tm=tm),
        grid=(bsz, s // tm),
        in_specs=[tok(d), _resident((d, n_in)), _resident((1, LANES)), _resident((1, LANES)),
                  _resident((LANES, LANES))] + [tab] * 6,
        out_specs=out_specs,
        out_shape=out_shape,
        scratch_shapes=scratch,
        compiler_params=_params("parallel", "parallel"),
        name="inproj1",
    )(x, w, qg, kg, ones_blk, *tabs_p, *tabs_a)


def _flash_kernel(*refs, mode, dv, tq, ck):
    if mode == "diff":
        (q_ref, k_ref, vt_ref, lq1_ref, lk1_ref, lq2_ref, lk2_ref, g_ref,
         o_ref, qs_ref, mb_ref, inv_ref, acc_ref, kmax_ref, ksc_ref, k8_ref, *p_refs) = refs
    else:
        (q_ref, k_ref, vt_ref,
         o_ref, qs_ref, mb_ref, inv_ref, acc_ref, kmax_ref, ksc_ref, k8_ref, *p_refs) = refs
    n_chunks = vt_ref.shape[0]
    n_stacks = k8_ref.shape[0]
    kv_head = pl.program_id(1) // 2
    lane = lax.broadcasted_iota(jnp.int32, (1, LANES), 1)
    lane_grp = lane // HEAD_DIM

    def chunk_rows(c):
        return pl.ds(pl.multiple_of(c * ck, ck), ck)

    def hi_lo(x):
        hi = x.astype(F8)
        return hi, (x - hi.astype(F32)).astype(F8)

    @pl.when(pl.program_id(2) == 0)
    def _():
        r = lax.broadcasted_iota(jnp.int32, (LANES, LANES), 0) // HEAD_DIM
        c_ = lax.broadcasted_iota(jnp.int32, (LANES, LANES), 1) // HEAD_DIM
        ones_blk = (r == c_).astype(BF16)

        def norm_body(c, mx):
            kf = k_ref[chunk_rows(c), :].astype(F32)
            return jnp.maximum(mx, jnp.max(_dot((kf * kf).astype(BF16), ones_blk), axis=0, keepdims=True))

        kmax = lax.fori_loop(0, n_chunks, norm_body, jnp.zeros((1, LANES), F32))
        kmax_ref[...] = kmax
        ksc = _pow2_scale(kmax)
        ksc_ref[...] = ksc

        def stack_body(c, carry):
            kf = k_ref[chunk_rows(c), :].astype(F32) * ksc
            swapped = pltpu.roll(kf, HEAD_DIM, 1)
            for st in range(n_stacks):
                grp = st if mode == "diff" else kv_head
                both = jnp.where((lane >= HEAD_DIM) == (grp == 1), kf, swapped)
                k8_ref[st, chunk_rows(c), :] = jnp.concatenate(hi_lo(both), axis=1)
            return carry

        lax.fori_loop(0, n_chunks, stack_body, 0)

    for m in range(2):
        q = q_ref[m * HEAD_DIM:(m + 1) * HEAD_DIM, :] if mode == "diff" else q_ref[m]
        grp = m if mode == "diff" else kv_head
        qf = q.astype(F32)
        qn2 = jnp.sum(qf * qf, axis=0, keepdims=True)
        in_grp = lane_grp == grp
        kmax2 = jnp.max(jnp.where(in_grp, kmax_ref[...], 0.0), axis=1, keepdims=True)
        ksc = jnp.max(jnp.where(in_grp, ksc_ref[...], 0.0), axis=1, keepdims=True)
        qsc = _pow2_scale(jnp.max(qn2, axis=1, keepdims=True))
        hi, lo = hi_lo(qf * qsc)
        qs_ref[m] = jnp.concatenate([hi, lo, hi, lo], axis=0)
        mb_ref[m] = jnp.sqrt(qn2 * kmax2) * BOUND_SLACK
        inv_ref[m] = jnp.broadcast_to(1.0 / (qsc * ksc), (1, tq))
    acc_ref[...] = jnp.zeros(acc_ref.shape, F32)

    def scores(rows, m):
        k8 = k8_ref[m if mode == "diff" else 0, rows, :]
        return _dot(k8, qs_ref[m]) * inv_ref[m]

    def stages(c_exp, slot_exp, c_pv, slot_pv):
        vt = None if c_pv is None else vt_ref[c_pv]
        for m in range(2):
            for n in range(tq // MXU_COLS):
                cols = slice(n * MXU_COLS, (n + 1) * MXU_COLS)
                if c_exp is not None:
                    k8 = k8_ref[m if mode == "diff" else 0, chunk_rows(c_exp), :]
                    s = _dot(k8, qs_ref[m, :, cols]) * inv_ref[m, :, cols]
                    p_refs[slot_exp][m, :, cols] = jnp.exp2(s - mb_ref[m, :, cols]).astype(BF16)
                if c_pv is not None:
                    acc_ref[m, :, cols] += _dot(vt, p_refs[slot_pv][m, :, cols])

    stages(0, 0, None, None)
    if n_chunks > 1:
        def pair_body(i, carry):
            stages(2 * i + 1, 1, 2 * i, 0)
            stages(2 * i + 2, 0, 2 * i + 1, 1)
            return carry

        lax.fori_loop(0, n_chunks // 2 - 1, pair_body, 0)
        stages(n_chunks - 1, 1, n_chunks - 2, 0)
    stages(None, None, n_chunks - 1, (n_chunks - 1) % 2)

    denom_min = jnp.min(jnp.minimum(acc_ref[0, dv:dv + 1, :], acc_ref[1, dv:dv + 1, :]))

    @pl.when(jnp.logical_not(denom_min >= SAFE_DENOM))
    def _():
        mb_ref[...] = jnp.full(mb_ref.shape, NEG_BIG, F32)
        acc_ref[...] = jnp.zeros(acc_ref.shape, F32)

        sub = min(ck, SLOW_KV_STEP)

        def slow_body(j, carry):
            start = pl.multiple_of(j * sub, sub)
            vt = vt_ref[start // ck, :, pl.ds(pl.multiple_of(start % ck, sub), sub)]
            for m in range(2):
                s = scores(pl.ds(start, sub), m)
                m_old = mb_ref[m]
                m_new = jnp.maximum(m_old, jnp.max(s, axis=0, keepdims=True))
                p = jnp.exp2(s - m_new).astype(BF16)
                acc_ref[m] = jnp.exp2(m_old - m_new) * acc_ref[m] + _dot(vt, p)
                mb_ref[m] = m_new
            return carry

        lax.fori_loop(0, n_chunks * (ck // sub), slow_body, 0)

    a0, a1 = acc_ref[0], acc_ref[1]
    o0 = a0[:dv] * (1.0 / a0[dv:dv + 1])
    o1 = a1[:dv] * (1.0 / a1[dv:dv + 1])
    if mode == "diff":
        lam = (jnp.exp(jnp.sum(lq1_ref[...] * lk1_ref[...], keepdims=True))
               - jnp.exp(jnp.sum(lq2_ref[...] * lk2_ref[...], keepdims=True)) + B_LAMBDA_INIT)
        o = o0 - lam * o1
        ms = jnp.mean(o * o, axis=0, keepdims=True)
        o = o * lax.rsqrt(ms + RMS_EPS) * g_ref[...] * (1.0 - B_LAMBDA_INIT)
    else:
        o = jnp.concatenate([o0, o1], axis=0)
    o_ref[...] = o.T.astype(o_ref.dtype)


def _flash(mode, qt, k, vt, extra, tq):
    bsz, s = k.shape[0], k.shape[1]
    n_chunks, dve, ck = vt.shape[2:]
    assert n_chunks == 1 or n_chunks % 2 == 0, "the two-slot pipeline walks key chunks in pairs"
    dv = dve - ONES_ROWS
    if mode == "diff":
        q_spec = pl.BlockSpec((None, None, 2 * HEAD_DIM, tq), lambda b, h, qi: (b, h, 0, qi))
        k_spec = pl.BlockSpec((None, s, LANES), lambda b, h, qi: (b, 0, h))
        vt_spec = pl.BlockSpec((None, None, n_chunks, dve, ck), lambda b, h, qi: (b, h, 0, 0, 0))
        extra_specs = [_resident((1, HEAD_DIM))] * 4 + [_resident((B_V_DIM, 1))]
    else:
        q_spec = pl.BlockSpec((None, None, 2, HEAD_DIM, tq), lambda b, h, qi: (b, h, 0, 0, qi))
        k_spec = pl.BlockSpec((None, s, LANES), lambda b, h, qi: (b, 0, 0))
        vt_spec = pl.BlockSpec((None, None, n_chunks, dve, ck), lambda b, h, qi: (b, h // 2, 0, 0, 0))
        extra_specs = []
    return pl.pallas_call(
        functools.partial(_flash_kernel, mode=mode, dv=dv, tq=tq, ck=ck),
        grid=(bsz, 4, s // tq),
        in_specs=[q_spec, k_spec, vt_spec] + extra_specs,
        out_specs=pl.BlockSpec((None, tq, LANES), lambda b, h, qi: (b, qi, h)),
        out_shape=jax.ShapeDtypeStruct((bsz, s, 512), BF16),
        scratch_shapes=[pltpu.VMEM((2, 4 * HEAD_DIM, tq), F8),
                        pltpu.VMEM((2, 1, tq), F32),
                        pltpu.VMEM((2, 1, tq), F32),
                        pltpu.VMEM((2, dve, tq), F32),
                        pltpu.VMEM((1, LANES), F32),
                        pltpu.VMEM((1, LANES), F32),
                        pltpu.VMEM((2 if mode == "diff" else 1, s, 4 * HEAD_DIM), F8),
                        pltpu.VMEM((2, ck, tq), BF16),
                        pltpu.VMEM((2, ck, tq), BF16)],
        compiler_params=_params("parallel", "parallel", "arbitrary"),
        name="flash_" + mode,
    )(qt, k, vt, *extra)


def _q_transposed(x, n_heads):
    bsz, s, width = x.shape
    return x.reshape(bsz, s, n_heads, width // n_heads).transpose(0, 2, 3, 1)


def _v_transposed(x, n_heads, ck):
    bsz, s, width = x.shape
    xt = x.reshape(bsz, s // ck, ck, n_heads, width // n_heads).transpose(0, 3, 1, 4, 2)
    return jnp.concatenate([xt, jnp.ones((bsz, n_heads, s // ck, ONES_ROWS, ck), x.dtype)], axis=3)


def _dilated_kernel(q_ref, kp_ref, km_ref, kn_ref, vp_ref, vm_ref, vn_ref,
                    o_ref, m_ref, l_ref, *, tq, sub_len):
    i0 = pl.program_id(2) * tq
    q = q_ref[...]
    kw = jnp.concatenate([kp_ref[...], km_ref[...], kn_ref[...]], axis=0)
    vw = jnp.concatenate([vp_ref[...], vm_ref[...], vn_ref[...]], axis=0)
    nk = tq + 2 * LANES
    t = lax.broadcasted_iota(jnp.int32, (tq, nk), 0)
    w = lax.broadcasted_iota(jnp.int32, (tq, nk), 1)
    off = w - LANES - t
    j = i0 - LANES + w
    valid = (off >= -D_HALF_WINDOW) & (off <= D_HALF_WINDOW) & (j >= 0) & (j < sub_len)
    lane = lax.broadcasted_iota(jnp.int32, (1, D_WIDTH), 1)
    o_acc = jnp.zeros((tq, D_WIDTH), F32)
    m_acc = jnp.zeros((tq, D_WIDTH), F32)
    l_acc = jnp.zeros((tq, D_WIDTH), F32)
    for h in range(D_SLOTS):
        in_head = (lane >= h * HEAD_DIM) & (lane < (h + 1) * HEAD_DIM)
        qh = jnp.where(in_head, q, jnp.zeros_like(q))
        s = lax.dot_general(qh, kw, (((1,), (1,)), ((), ())), preferred_element_type=F32)
        s = jnp.where(valid, s, NEG_BIG)
        mh = jnp.max(s, axis=1, keepdims=True)
        p = jnp.exp2(s - mh)
        lh = jnp.sum(p, axis=1, keepdims=True)
        oh = _dot(p.astype(BF16), vw)
        sel = in_head.astype(F32)
        o_acc = o_acc + oh * sel
        m_acc = m_acc + mh * sel
        l_acc = l_acc + lh * sel
    o_ref[...] = o_acc
    m_ref[...] = m_acc
    l_ref[...] = l_acc


def _dilated(qd, kd, vd, pat):
    bsz, dilation, sub_len, _ = qd.shape
    tq = min(256, sub_len)
    blk128 = sub_len // LANES
    per_tile = tq // LANES
    main = pl.BlockSpec((None, None, tq, D_WIDTH), lambda b, c, i: (b, c, i, 0))
    prev = pl.BlockSpec((None, None, LANES, D_WIDTH),
                        lambda b, c, i: (b, c, jnp.maximum(i * per_tile - 1, 0), 0))
    nxt = pl.BlockSpec((None, None, LANES, D_WIDTH),
                       lambda b, c, i: (b, c, jnp.minimum((i + 1) * per_tile, blk128 - 1), 0))
    out = jax.ShapeDtypeStruct((bsz, dilation, sub_len, D_WIDTH), F32)
    return pl.pallas_call(
        functools.partial(_dilated_kernel, tq=tq, sub_len=sub_len),
        grid=(bsz, dilation, sub_len // tq),
        in_specs=[main, prev, main, nxt, prev, main, nxt],
        out_specs=[main] * 3,
        out_shape=[out] * 3,
        compiler_params=_params("parallel", "parallel", "parallel"),
        name="dilated_p%d" % pat,
    )(qd, kd, kd, kd, vd, vd, vd)


def _mix0_kernel(x_ref, u_ref, vn_ref, yb_ref, ws_ref, bs_ref, w_ref, g_ref, b_ref, o_ref, *, tm):
    lane = lax.broadcasted_iota(jnp.int32, (A_CHUNK, LANES), 1)
    left = lane < HEAD_DIM
    rows = []
    for n in range(tm // A_CHUNK):
        r = slice(n * A_CHUNK, (n + 1) * A_CHUNK)
        cols = []
        for j in range(A_WIDTH // LANES):
            v = vn_ref[r, j * LANES:(j + 1) * LANES]
            zero = jnp.zeros_like(v)
            stacked = jnp.concatenate([jnp.where(left, v, zero), jnp.where(left, zero, v)], axis=0)
            cols.append(_dot(ws_ref[j], stacked))
        rows.append(jnp.concatenate(cols, axis=1) + bs_ref[...])
    mixed = jnp.concatenate(rows, axis=0)
    ya = (u_ref[...] * mixed).astype(BF16)
    y = _dot(ya, w_ref[0:A_WIDTH, :]) + _dot(yb_ref[...], w_ref[A_WIDTH:, :])
    o_ref[...] = _layer_norm(ALPHA * x_ref[...] + y, g_ref[...], b_ref[...])


def _mix0(x, u, vn, yb, ws_pair, bs_x, w_out, g, b, tm):
    bsz, s, d = x.shape
    tok = lambda width: pl.BlockSpec((None, tm, width), lambda bb, i: (bb, i, 0))
    return pl.pallas_call(
        functools.partial(_mix0_kernel, tm=tm),
        grid=(bsz, s // tm),
        in_specs=[tok(d), tok(512), tok(512), tok(512), _resident(ws_pair.shape), _resident(bs_x.shape),
                  _resident(w_out.shape), _resident((1, d)), _resident((1, d))],
        out_specs=tok(d),
        out_shape=jax.ShapeDtypeStruct((bsz, s, d), F32),
        compiler_params=_params("parallel", "parallel"),
        name="mix0",
    )(x, u, vn, yb, ws_pair, bs_x, w_out, g, b)


def _mix1_kernel(x_ref, yc_ref, *refs, tm):
    pat_refs, (w_ref, g_ref, b_ref, o_ref), scr_refs = refs[:9], refs[9:13], refs[13:]

    def token_major(idx):
        dil = D_PATTERNS[idx // 3][1]
        ref = pat_refs[idx]
        if dil == 1:
            return ref[0]
        scr = scr_refs[idx - 3]
        for c in range(dil):
            for j in range(D_WIDTH // LANES):
                scr[j, pl.ds(c, tm // dil, stride=dil), :] = ref[c, :, j * LANES:(j + 1) * LANES]
        return jnp.concatenate([scr[j] for j in range(D_WIDTH // LANES)], axis=1)

    vals = [token_major(i) for i in range(9)]
    os_, ms_, ls_ = vals[0::3], vals[1::3], vals[2::3]
    m_all = jnp.maximum(jnp.maximum(ms_[0], ms_[1]), ms_[2])
    num = jnp.zeros(m_all.shape, F32)
    den = jnp.zeros(m_all.shape, F32)
    for p in range(3):
        wgt = jnp.exp2(ms_[p] - m_all)
        num = num + wgt * os_[p]
        den = den + wgt * ls_[p]
    yd = (num * (1.0 / den)).astype(BF16)
    y = _dot(yc_ref[...], w_ref[0:512, :]) + _dot(yd, w_ref[512:, :])
    o_ref[...] = _layer_norm(ALPHA * x_ref[...] + y, g_ref[...], b_ref[...])


def _mix1(x, yc, pats, w_out, g, b, tm):
    bsz, s, d = x.shape
    tok = lambda width: pl.BlockSpec((None, tm, width), lambda bb, i: (bb, i, 0))
    flat = [a for oml in pats for a in oml]
    pat_specs = [pl.BlockSpec((None, a.shape[1], tm // a.shape[1], D_WIDTH), lambda bb, i: (bb, 0, i, 0))
                 for a in flat]
    return pl.pallas_call(
        functools.partial(_mix1_kernel, tm=tm),
        grid=(bsz, s // tm),
        in_specs=[tok(d), tok(512)] + pat_specs
                 + [_resident(w_out.shape), _resident((1, d)), _resident((1, d))],
        out_specs=tok(d),
        out_shape=jax.ShapeDtypeStruct((bsz, s, d), F32),
        scratch_shapes=[pltpu.VMEM((D_WIDTH // LANES, tm, LANES), F32)] * 6,
        compiler_params=_params("parallel", "parallel"),
        name="mix1",
    )(x, yc, *flat, w_out, g, b)


def _ffn_kernel(xp_ref, xm_ref, xn_ref, wa_ref, wg_ref, ca_ref, cg_ref, wd_ref, g_ref, b_ref,
                o_ref, xe_ref, *, tm, tiles_per_seq):
    i = pl.program_id(0) % tiles_per_seq
    halo = 8
    prev = jnp.where(i > 0, xp_ref[...], 0.0)
    nxt = jnp.where(i < tiles_per_seq - 1, xn_ref[...], 0.0)
    xe_ref[...] = jnp.concatenate([prev, xm_ref[...], nxt], axis=0).astype(BF16)
    ext = tm + 2 * halo

    def conv(h, cw):
        mid = slice(halo, halo + tm)
        return (cw[0:1] * pltpu.roll(h, 1, 0)[mid] + cw[1:2] * h[mid]
                + cw[2:3] * pltpu.roll(h, ext - 1, 0)[mid] + cw[3:4])

    n_c = wa_ref.shape[0]
    y = None
    acts = []
    for c in range(n_c):
        xe = xe_ref[...]
        a = conv(_dot(xe, wa_ref[c]), ca_ref[c])
        gt = conv(_dot(xe, wg_ref[c]), cg_ref[c])
        acts.append((jax.nn.gelu(gt) * a).astype(BF16))
        if len(acts) == FF_DOWN_GROUP or c == n_c - 1:
            c0 = c + 1 - len(acts)
            part = _dot(jnp.concatenate(acts, axis=1), wd_ref[c0 * FF_CHUNK:(c + 1) * FF_CHUNK, :])
            y = part if y is None else y + part
            acts = []
    o_ref[...] = _layer_norm(ALPHA * xm_ref[...] + y, g_ref[...], b_ref[...])


def _ffn(x, wa, wg, ca, cg, wd, g, b, tm):
    bsz, s, d = x.shape
    n_tok = bsz * s
    xf = x.reshape(n_tok, d)
    tiles_per_seq = s // tm
    r8 = tm // 8
    n8 = n_tok // 8
    out = pl.pallas_call(
        functools.partial(_ffn_kernel, tm=tm, tiles_per_seq=tiles_per_seq),
        grid=(n_tok // tm,),
        in_specs=[pl.BlockSpec((8, d), lambda i: (jnp.maximum(i * r8 - 1, 0), 0)),
                  pl.BlockSpec((tm, d), lambda i: (i, 0)),
                  pl.BlockSpec((8, d), lambda i: (jnp.minimum((i + 1) * r8, n8 - 1), 0)),
                  _resident(wa.shape), _resident(wg.shape), _resident(ca.shape), _resident(cg.shape),
                  _resident(wd.shape), _resident((1, d)), _resident((1, d))],
        out_specs=pl.BlockSpec((tm, d), lambda i: (i, 0)),
        out_shape=jax.ShapeDtypeStruct((n_tok, d), F32),
        scratch_shapes=[pltpu.VMEM((tm + 16, d), BF16)],
        compiler_params=_params("parallel"),
        name="conv_ffn",
    )(xf, xf, xf, wa, wg, ca, cg, wd, g, b)
    return out.reshape(bsz, s, d)


def _rope_tables(pos, n_dims, theta):
    inv = jnp.power(theta, -jnp.arange(0, n_dims, 2, dtype=F32) / n_dims)
    ang = pos.astype(F32)[:, None] * inv[None, :]
    return jnp.cos(ang), jnp.sin(ang)


def _lane_tables(parts, s):
    cs, lo, hi = [], [], []
    zero = None
    for cos, sin in parts:
        zero = jnp.zeros_like(sin)
        cs += [cos, cos]
        lo += [-sin, zero]
        hi += [zero, sin]
    used = sum(a.shape[1] for a in cs)
    pad1 = jnp.ones((s, HEAD_DIM - used), F32)
    pad0 = jnp.zeros((s, HEAD_DIM - used), F32)
    build = lambda xs, pad: jnp.tile(jnp.concatenate(xs + [pad], axis=1), (1, LANES // HEAD_DIM))
    return build(cs, pad1), build(lo, pad0), build(hi, pad0)


def _ffn_weights(w_up, conv_w, conv_b, w_down):
    d = w_up.shape[0]
    n_c = D_FF // FF_CHUNK
    chunks = lambda w: w.reshape(d, n_c, FF_CHUNK).transpose(1, 0, 2).astype(BF16)
    wa, wg = chunks(w_up[:, :D_FF]), chunks(w_up[:, D_FF:])

    def conv_pack(cw, cb):
        rows = jnp.concatenate([cw, cb[None, :], jnp.zeros((4, D_FF), F32)], axis=0)
        return rows.reshape(8, n_c, FF_CHUNK).transpose(1, 0, 2)

    ca = conv_pack(conv_w[:, :D_FF], conv_b[:D_FF])
    cg = conv_pack(conv_w[:, D_FF:], conv_b[D_FF:])
    wd = w_down.astype(BF16)
    return wa, wg, ca, cg, wd


def _trunk(x, p, tm=512, tq=1024):
    bsz, s, d = x.shape
    tm, tq, ck = min(tm, s), min(tq, s), min(KV_CHUNK, s)
    pos = jnp.arange(s)
    tabs_p = _lane_tables([_rope_tables(pos, ROPE_DIMS, ROPE_THETA)], s)
    tabs_a = _lane_tables([_rope_tables(pos // GRID_W, HEAD_DIM // 2, AXIAL_THETA),
                           _rope_tables(pos % GRID_W, HEAD_DIM // 2, AXIAL_THETA)], s)
    row = lambda v: v.reshape(1, -1)

    u, vn, q0, k0, v0 = _inproj0(x, p["w_in0"], row(p["a_ln_g"]), row(p["a_ln_b"]), tabs_p, tm)
    yb = _flash("diff", _q_transposed(q0, B_HEADS), k0, _v_transposed(v0, B_HEADS, ck),
                [row(p["b_lam_q1"]), row(p["b_lam_k1"]), row(p["b_lam_q2"]), row(p["b_lam_k2"]),
                 p["b_subln_g"].reshape(-1, 1)], tq)
    x = _mix0(x, u, vn, yb, p["ws_pair"], p["bs_x"], p["w_out0"], row(p["ln0a_g"]), row(p["ln0a_b"]), tm)
    x = _ffn(x, *p["ffn0"], row(p["ln0b_g"]), row(p["ln0b_b"]), tm)

    qc, kc, vc, *qkv_d = _inproj1(x, p["w_in1"], p["qg"], p["kg"], p["ones_blk"], tabs_p, tabs_a, tm)
    yc = _flash("gqa", _q_transposed(qc, C_Q_HEADS).reshape(bsz, 4, 2, HEAD_DIM, s), kc,
                _v_transposed(vc, C_KV_HEADS, ck), [], tq)
    pats = [_dilated(*qkv_d[3 * pi:3 * pi + 3], pi) for pi in range(len(D_PATTERNS))]
    x = _mix1(x, yc, pats, p["w_out1"], row(p["ln1a_g"]), row(p["ln1a_b"]), tm)
    x = _ffn(x, *p["ffn1"], row(p["ln1b_g"]), row(p["ln1b_b"]), tm)
    return x


def _prepare(w_in0, a_ln_g, a_ln_b, a_ws, a_bs, b_lam_q1, b_lam_k1, b_lam_q2, b_lam_k2, b_subln_g,
             w_out0, ln0a_g, ln0a_b, w_up0, conv_w0, conv_b0, w_down0, ln0b_g, ln0b_b,
             w_in1, c_qnorm_g, c_knorm_g, w_out1, ln1a_g, ln1a_b, w_up1, conv_w1, conv_b1, w_down1,
             ln1b_g, ln1b_b):
    grp = jnp.arange(LANES) // HEAD_DIM
    return dict(
        w_in0=w_in0.astype(BF16), a_ln_g=a_ln_g, a_ln_b=a_ln_b,
        ws_pair=a_ws.reshape(A_GROUPS // 2, 2, A_CHUNK, A_CHUNK).transpose(0, 2, 1, 3)
        .reshape(A_GROUPS // 2, A_CHUNK, 2 * A_CHUNK).astype(BF16),
        bs_x=jnp.repeat(a_bs.T, HEAD_DIM, axis=1),
        b_lam_q1=b_lam_q1, b_lam_k1=b_lam_k1, b_lam_q2=b_lam_q2, b_lam_k2=b_lam_k2, b_subln_g=b_subln_g,
        w_out0=w_out0.astype(BF16), ln0a_g=ln0a_g, ln0a_b=ln0a_b,
        ffn0=_ffn_weights(w_up0, conv_w0, conv_b0, w_down0), ln0b_g=ln0b_g, ln0b_b=ln0b_b,
        w_in1=w_in1.astype(BF16),
        qg=jnp.tile(c_qnorm_g, LANES // HEAD_DIM).reshape(1, LANES),
        kg=jnp.tile(c_knorm_g, LANES // HEAD_DIM).reshape(1, LANES),
        ones_blk=(grp[:, None] == grp[None, :]).astype(BF16),
        w_out1=w_out1.astype(BF16), ln1a_g=ln1a_g, ln1a_b=ln1a_b,
        ffn1=_ffn_weights(w_up1, conv_w1, conv_b1, w_down1), ln1b_g=ln1b_g, ln1b_b=ln1b_b,
    )


def kernel(x_prompt, x_sample, w_in0, a_ln_g, a_ln_b, a_ws, a_bs, b_lam_q1, b_lam_k1, b_lam_q2, b_lam_k2, b_subln_g, w_out0, ln0a_g, ln0a_b, w_up0, conv_w0, conv_b0, w_down0, ln0b_g, ln0b_b, w_in1, c_qnorm_g, c_knorm_g, w_out1, ln1a_g, ln1a_b, w_up1, conv_w1, conv_b1, w_down1, ln1b_g, ln1b_b):
    p = _prepare(w_in0, a_ln_g, a_ln_b, a_ws, a_bs, b_lam_q1, b_lam_k1, b_lam_q2, b_lam_k2, b_subln_g,
                 w_out0, ln0a_g, ln0a_b, w_up0, conv_w0, conv_b0, w_down0, ln0b_g, ln0b_b,
                 w_in1, c_qnorm_g, c_knorm_g, w_out1, ln1a_g, ln1a_b, w_up1, conv_w1, conv_b1, w_down1,
                 ln1b_g, ln1b_b)
    return (_trunk(x_prompt, p), _trunk(x_sample, p))
```

```python
import functools
import math

import jax
import jax.numpy as jnp
from jax import lax
from jax.experimental import pallas as pl
from jax.experimental.pallas import tpu as pltpu

F32 = jnp.float32
BF16 = jnp.bfloat16
F8 = jnp.float8_e4m3fn
F8_TARGET = 128.0
TINY_NORM_SQ = 1e-30
MAX_SCALE_EXP = 40.0

D_MODEL = 1024
DEPTH = 2
HEAD_DIM = 64
GRID_W = 64
LN_EPS = 1e-5
RMS_EPS = 1e-6
ROPE_THETA = 500000.0
ROPE_DIMS = HEAD_DIM // 4
AXIAL_THETA = 10000.0
A_GROUPS = 8
A_WIDTH = A_GROUPS * HEAD_DIM
A_CHUNK = 128
B_HEADS = 4
B_V_DIM = 2 * HEAD_DIM
B_LAMBDA_INIT = 0.8 - 0.6 * math.exp(-0.3 * 0)
C_Q_HEADS = 8
C_KV_HEADS = 2
D_PATTERNS = ((128, 1), (512, 4), (2048, 16))
D_SLOTS = 4
D_WIDTH = D_SLOTS * HEAD_DIM
D_HALF_WINDOW = 64
D_FF = 2816
FF_CHUNK = 256
FF_DOWN_GROUP = 4
ALPHA = (2 * DEPTH) ** 0.25
QK_SCALE = HEAD_DIM ** -0.5 * math.log2(math.e)
NEG_BIG = -1e30
BOUND_SLACK = 1.0 + 2.0 ** -6
SAFE_DENOM = 2.0 ** -64
KV_CHUNK = 1024
SLOW_KV_STEP = 256

LANES = 128
MXU_COLS = 256
ONES_ROWS = 16
VMEM_LIMIT = 56 * 1024 * 1024


def _params(*sem):
    return pltpu.CompilerParams(dimension_semantics=sem, vmem_limit_bytes=VMEM_LIMIT)


def _resident(shape):
    nd = len(shape)
    return pl.BlockSpec(shape, lambda *_: (0,) * nd, pipeline_mode=pl.Buffered(1))


def _layer_norm(x, g, b):
    mu = jnp.mean(x, -1, keepdims=True)
    xc = x - mu
    var = jnp.mean(xc * xc, -1, keepdims=True)
    return xc * lax.rsqrt(var + LN_EPS) * g + b


def _rope(x, c, s_lo, s_hi, shift):
    return (x * c + pltpu.roll(x, LANES - shift, 1) * s_lo + pltpu.roll(x, shift, 1) * s_hi)


def _dot(a, b):
    return jnp.dot(a, b, preferred_element_type=F32)


def _pow2_scale(max_sq):
    norm = jnp.sqrt(jnp.maximum(max_sq, TINY_NORM_SQ))
    return jnp.exp2(jnp.minimum(jnp.floor(jnp.log2(F8_TARGET / norm)), MAX_SCALE_EXP))


def _inproj0_kernel(x_ref, w_ref, lng_ref, lnb_ref, c_ref, s1_ref, s2_ref,
                    u_ref, vn_ref, qt_ref, k_ref, vt_ref):
    tm = x_ref.shape[0]
    xb = x_ref[...].astype(BF16)
    za = jax.nn.gelu(_dot(xb, w_ref[:, 0:2 * A_WIDTH]))
    u_ref[...] = za[:, :A_WIDTH]
    vn_ref[...] = _layer_norm(za[:, A_WIDTH:], lng_ref[...], lnb_ref[...]).astype(BF16)
    c, s1, s2 = c_ref[...], s1_ref[...], s2_ref[...]
    base = 2 * A_WIDTH
    hq = _dot(xb, w_ref[:, base:base + 512])
    hk = _dot(xb, w_ref[:, base + 512:base + 1024])
    hv = _dot(xb, w_ref[:, base + 1024:base + 1536])
    ones = jnp.ones((ONES_ROWS, tm), BF16)
    for j in range(B_HEADS):
        sl = slice(j * LANES, (j + 1) * LANES)
        qt_ref[j] = (_rope(hq[:, sl], c, s1, s2, ROPE_DIMS // 2) * QK_SCALE).T.astype(BF16)
        k_ref[:, sl] = _rope(hk[:, sl], c, s1, s2, ROPE_DIMS // 2).astype(BF16)
        vt_ref[j, 0:B_V_DIM, :] = hv[:, sl].T.astype(BF16)
        vt_ref[j, B_V_DIM:, :] = ones


def _vt_spec(n_heads, rows, tm, ck):
    per_chunk = ck // tm
    return pl.BlockSpec((None, n_heads, None, rows, tm),
                        lambda b, i: (b, 0, i // per_chunk, 0, i % per_chunk))


def _inproj0(x, w, lng, lnb, tabs, tm, ck):
    bsz, s, d = x.shape
    n_in = w.shape[1]
    dve = B_V_DIM + ONES_ROWS
    tok = lambda width: pl.BlockSpec((None, tm, width), lambda b, i: (b, i, 0))
    tab = pl.BlockSpec((tm, LANES), lambda b, i: (i, 0))
    return pl.pallas_call(
        _inproj0_kernel,
        grid=(bsz, s // tm),
        in_specs=[tok(d), _resident((d, n_in)), _resident((1, 512)), _resident((1, 512)), tab, tab, tab],
        out_specs=[tok(512), tok(512),
                   pl.BlockSpec((None, B_HEADS, 2 * HEAD_DIM, tm), lambda b, i: (b, 0, 0, i)),
                   tok(512), _vt_spec(B_HEADS, dve, tm, ck)],
        out_shape=[jax.ShapeDtypeStruct((bsz, s, 512), F32), jax.ShapeDtypeStruct((bsz, s, 512), BF16),
                   jax.ShapeDtypeStruct((bsz, B_HEADS, 2 * HEAD_DIM, s), BF16),
                   jax.ShapeDtypeStruct((bsz, s, 512), BF16),
                   jax.ShapeDtypeStruct((bsz, B_HEADS, s // ck, dve, ck), BF16)],
        compiler_params=_params("parallel", "parallel"),
        name="inproj0",
    )(x, w, lng, lnb, *tabs)


def _group_rms(x, gain, ones_blk):
    x2 = x * x
    hi = x2.astype(BF16)
    lo = (x2 - hi.astype(F32)).astype(BF16)
    ss = _dot(hi, ones_blk) + _dot(lo, ones_blk)
    return x * lax.rsqrt(ss * (1.0 / HEAD_DIM) + RMS_EPS) * gain


def _inproj1_kernel(x_ref, w_ref, qg_ref, kg_ref, ones_ref, c_ref, s1_ref, s2_ref,
                    ca_ref, sa1_ref, sa2_ref,
                    qc_ref, kc_ref, vc_ref, *rest, tm):
    d_refs, scr_refs = rest[:9], rest[9:]

    def emit_classes(kind, vals):
        for pi, (_, dil) in enumerate(D_PATTERNS):
            out_ref = d_refs[3 * pi + kind]
            blk = vals[:, pi * D_WIDTH:(pi + 1) * D_WIDTH]
            if dil == 1:
                out_ref[0] = blk.astype(BF16)
                continue
            scr = scr_refs[3 * (pi - 1) + kind]
            for j in range(D_WIDTH // LANES):
                scr[j] = blk[:, j * LANES:(j + 1) * LANES]
            for c in range(dil):
                out_ref[c] = jnp.concatenate(
                    [scr[j, pl.ds(c, tm // dil, stride=dil), :] for j in range(D_WIDTH // LANES)],
                    axis=1).astype(BF16)

    xb = x_ref[...].astype(BF16)
    ones_blk = ones_ref[...]
    ca, sa1, sa2 = ca_ref[...], sa1_ref[...], sa2_ref[...]
    hq = _dot(xb, w_ref[:, 0:512])
    for j in range(C_Q_HEADS // 2):
        sl = slice(j * LANES, (j + 1) * LANES)
        qn = _group_rms(hq[:, sl], qg_ref[...], ones_blk)
        qt = (_rope(qn, ca, sa1, sa2, HEAD_DIM // 4) * QK_SCALE).T.astype(BF16)
        qc_ref[j, 0] = qt[:HEAD_DIM]
        qc_ref[j, 1] = qt[HEAD_DIM:]
    hkv = _dot(xb, w_ref[:, 512:768])
    kn = _group_rms(hkv[:, :LANES], kg_ref[...], ones_blk)
    kc_ref[...] = _rope(kn, ca, sa1, sa2, HEAD_DIM // 4).astype(BF16)
    vt = hkv[:, LANES:].T.astype(BF16)
    ones = jnp.ones((ONES_ROWS, tm), BF16)
    for g in range(C_KV_HEADS):
        vc_ref[g, 0:HEAD_DIM, :] = vt[g * HEAD_DIM:(g + 1) * HEAD_DIM]
        vc_ref[g, HEAD_DIM:, :] = ones
    c, s1, s2 = c_ref[...], s1_ref[...], s2_ref[...]
    hqd = _dot(xb, w_ref[:, 768:1536])
    hkd = _dot(xb, w_ref[:, 1536:2304])
    blocks = range(3 * D_WIDTH // LANES)
    lane_block = lambda j: slice(j * LANES, (j + 1) * LANES)
    emit_classes(0, jnp.concatenate(
        [_rope(hqd[:, lane_block(j)], c, s1, s2, ROPE_DIMS // 2) * QK_SCALE for j in blocks], axis=1))
    emit_classes(1, jnp.concatenate(
        [_rope(hkd[:, lane_block(j)], c, s1, s2, ROPE_DIMS // 2) for j in blocks], axis=1))
    emit_classes(2, _dot(xb, w_ref[:, 2304:3072]))


def _inproj1(x, w, qg, kg, ones_blk, tabs_p, tabs_a, tm, ck):
    bsz, s, d = x.shape
    n_in = w.shape[1]
    dve = HEAD_DIM + ONES_ROWS
    tok = lambda width: pl.BlockSpec((None, tm, width), lambda b, i: (b, i, 0))
    tab = pl.BlockSpec((tm, LANES), lambda b, i: (i, 0))
    out_specs = [pl.BlockSpec((None, C_Q_HEADS // 2, 2, HEAD_DIM, tm), lambda b, i: (b, 0, 0, 0, i)),
                 tok(LANES), _vt_spec(C_KV_HEADS, dve, tm, ck)]
    out_shape = [jax.ShapeDtypeStruct((bsz, C_Q_HEADS // 2, 2, HEAD_DIM, s), BF16),
                 jax.ShapeDtypeStruct((bsz, s, LANES), BF16),
                 jax.ShapeDtypeStruct((bsz, C_KV_HEADS, s // ck, dve, ck), BF16)]
    scratch = []
    for _, dil in D_PATTERNS:
        out_specs += [pl.BlockSpec((None, dil, tm // dil, D_WIDTH), lambda b, i: (b, 0, i, 0))] * 3
        out_shape += [jax.ShapeDtypeStruct((bsz, dil, s // dil, D_WIDTH), BF16)] * 3
        if dil > 1:
            scratch += [pltpu.VMEM((D_WIDTH // LANES, tm, LANES), F32)] * 3
    return pl.pallas_call(
        functools.partial(_inproj1_kernel, tm=tm),
        grid=(bsz, s // tm),
        in_specs=[tok(d), _resident((d, n_in)), _resident((1, LANES)), _resident((1, LANES)),
                  _resident((LANES, LANES))] + [tab] * 6,
        out_specs=out_specs,
        out_shape=out_shape,
        scratch_shapes=scratch,
        compiler_params=_params("parallel", "parallel"),
        name="inproj1",
    )(x, w, qg, kg, ones_blk, *tabs_p, *tabs_a)


def _flash_kernel(*refs, mode, dv, tq, ck):
    if mode == "diff":
        (q_ref, k_ref, vt_ref, lq1_ref, lk1_ref, lq2_ref, lk2_ref, g_ref,
         o_ref, qs_ref, mb_ref, inv_ref, acc_ref, kmax_ref, ksc_ref, k8_ref, *p_refs) = refs
    else:
        (q_ref, k_ref, vt_ref,
         o_ref, qs_ref, mb_ref, inv_ref, acc_ref, kmax_ref, ksc_ref, k8_ref, *p_refs) = refs
    n_chunks = vt_ref.shape[0]
    n_stacks = k8_ref.shape[0]
    kv_head = pl.program_id(1) // 2
    lane = lax.broadcasted_iota(jnp.int32, (1, LANES), 1)
    lane_grp = lane // HEAD_DIM

    def chunk_rows(c):
        return pl.ds(pl.multiple_of(c * ck, ck), ck)

    def hi_lo(x):
        hi = x.astype(F8)
        return hi, (x - hi.astype(F32)).astype(F8)

    @pl.when(pl.program_id(2) == 0)
    def _():
        r = lax.broadcasted_iota(jnp.int32, (LANES, LANES), 0) // HEAD_DIM
        c_ = lax.broadcasted_iota(jnp.int32, (LANES, LANES), 1) // HEAD_DIM
        ones_blk = (r == c_).astype(BF16)

        def norm_body(c, mx):
            kf = k_ref[chunk_rows(c), :].astype(F32)
            return jnp.maximum(mx, jnp.max(_dot((kf * kf).astype(BF16), ones_blk), axis=0, keepdims=True))

        kmax = lax.fori_loop(0, n_chunks, norm_body, jnp.zeros((1, LANES), F32))
        kmax_ref[...] = kmax
        ksc = _pow2_scale(kmax)
        ksc_ref[...] = ksc

        def stack_body(c, carry):
            kf = k_ref[chunk_rows(c), :].astype(F32) * ksc
            swapped = pltpu.roll(kf, HEAD_DIM, 1)
            for st in range(n_stacks):
                grp = st if mode == "diff" else kv_head
                both = jnp.where((lane >= HEAD_DIM) == (grp == 1), kf, swapped)
                k8_ref[st, chunk_rows(c), :] = jnp.concatenate(hi_lo(both), axis=1)
            return carry

        lax.fori_loop(0, n_chunks, stack_body, 0)

    for m in range(2):
        q = q_ref[m * HEAD_DIM:(m + 1) * HEAD_DIM, :] if mode == "diff" else q_ref[m]
        grp = m if mode == "diff" else kv_head
        qf = q.astype(F32)
        qn2 = jnp.sum(qf * qf, axis=0, keepdims=True)
        in_grp = lane_grp == grp
        kmax2 = jnp.max(jnp.where(in_grp, kmax_ref[...], 0.0), axis=1, keepdims=True)
        ksc = jnp.max(jnp.where(in_grp, ksc_ref[...], 0.0), axis=1, keepdims=True)
        qsc = _pow2_scale(jnp.max(qn2, axis=1, keepdims=True))
        hi, lo = hi_lo(qf * qsc)
        qs_ref[m] = jnp.concatenate([hi, lo, hi, lo], axis=0)
        mb_ref[m] = jnp.sqrt(qn2 * kmax2) * BOUND_SLACK
        inv_ref[m] = jnp.broadcast_to(1.0 / (qsc * ksc), (1, tq))
    acc_ref[...] = jnp.zeros(acc_ref.shape, F32)

    def scores(rows, m):
        k8 = k8_ref[m if mode == "diff" else 0, rows, :]
        return _dot(k8, qs_ref[m]) * inv_ref[m]

    def stages(c_exp, slot_exp, c_pv, slot_pv):
        vt = None if c_pv is None else vt_ref[c_pv]
        for m in range(2):
            for n in range(tq // MXU_COLS):
                cols = slice(n * MXU_COLS, (n + 1) * MXU_COLS)
                if c_exp is not None:
                    k8 = k8_ref[m if mode == "diff" else 0, chunk_rows(c_exp), :]
                    s = _dot(k8, qs_ref[m, :, cols]) * inv_ref[m, :, cols]
                    p_refs[slot_exp][m, :, cols] = jnp.exp2(s - mb_ref[m, :, cols]).astype(BF16)
                if c_pv is not None:
                    acc_ref[m, :, cols] += _dot(vt, p_refs[slot_pv][m, :, cols])

    stages(0, 0, None, None)
    if n_chunks > 1:
        def pair_body(i, carry):
            stages(2 * i + 1, 1, 2 * i, 0)
            stages(2 * i + 2, 0, 2 * i + 1, 1)
            return carry

        lax.fori_loop(0, n_chunks // 2 - 1, pair_body, 0)
        stages(n_chunks - 1, 1, n_chunks - 2, 0)
    stages(None, None, n_chunks - 1, (n_chunks - 1) % 2)

    denom_min = jnp.min(jnp.minimum(acc_ref[0, dv:dv + 1, :], acc_ref[1, dv:dv + 1, :]))

    @pl.when(jnp.logical_not(denom_min >= SAFE_DENOM))
    def _():
        mb_ref[...] = jnp.full(mb_ref.shape, NEG_BIG, F32)
        acc_ref[...] = jnp.zeros(acc_ref.shape, F32)

        sub = min(ck, SLOW_KV_STEP)

        def slow_body(j, carry):
            start = pl.multiple_of(j * sub, sub)
            vt = vt_ref[start // ck, :, pl.ds(pl.multiple_of(start % ck, sub), sub)]
            for m in range(2):
                s = scores(pl.ds(start, sub), m)
                m_old = mb_ref[m]
                m_new = jnp.maximum(m_old, jnp.max(s, axis=0, keepdims=True))
                p = jnp.exp2(s - m_new).astype(BF16)
                acc_ref[m] = jnp.exp2(m_old - m_new) * acc_ref[m] + _dot(vt, p)
                mb_ref[m] = m_new
            return carry

        lax.fori_loop(0, n_chunks * (ck // sub), slow_body, 0)

    a0, a1 = acc_ref[0], acc_ref[1]
    o0 = a0[:dv] * (1.0 / a0[dv:dv + 1])
    o1 = a1[:dv] * (1.0 / a1[dv:dv + 1])
    if mode == "diff":
        lam = (jnp.exp(jnp.sum(lq1_ref[...] * lk1_ref[...], keepdims=True))
               - jnp.exp(jnp.sum(lq2_ref[...] * lk2_ref[...], keepdims=True)) + B_LAMBDA_INIT)
        o = o0 - lam * o1
        ms = jnp.mean(o * o, axis=0, keepdims=True)
        o = o * lax.rsqrt(ms + RMS_EPS) * g_ref[...] * (1.0 - B_LAMBDA_INIT)
    else:
        o = jnp.concatenate([o0, o1], axis=0)
    o_ref[...] = o.T.astype(o_ref.dtype)


def _flash(mode, qt, k, vt, extra, tq):
    bsz, s = k.shape[0], k.shape[1]
    n_chunks, dve, ck = vt.shape[2:]
    assert n_chunks == 1 or n_chunks % 2 == 0, "the two-slot pipeline walks key chunks in pairs"
    dv = dve - ONES_ROWS
    if mode == "diff":
        q_spec = pl.BlockSpec((None, None, 2 * HEAD_DIM, tq), lambda b, h, qi: (b, h, 0, qi))
        k_spec = pl.BlockSpec((None, s, LANES), lambda b, h, qi: (b, 0, h))
        vt_spec = pl.BlockSpec((None, None, n_chunks, dve, ck), lambda b, h, qi: (b, h, 0, 0, 0))
        extra_specs = [_resident((1, HEAD_DIM))] * 4 + [_resident((B_V_DIM, 1))]
    else:
        q_spec = pl.BlockSpec((None, None, 2, HEAD_DIM, tq), lambda b, h, qi: (b, h, 0, 0, qi))
        k_spec = pl.BlockSpec((None, s, LANES), lambda b, h, qi: (b, 0, 0))
        vt_spec = pl.BlockSpec((None, None, n_chunks, dve, ck), lambda b, h, qi: (b, h // 2, 0, 0, 0))
        extra_specs = []
    return pl.pallas_call(
        functools.partial(_flash_kernel, mode=mode, dv=dv, tq=tq, ck=ck),
        grid=(bsz, 4, s // tq),
        in_specs=[q_spec, k_spec, vt_spec] + extra_specs,
        out_specs=pl.BlockSpec((None, tq, LANES), lambda b, h, qi: (b, qi, h)),
        out_shape=jax.ShapeDtypeStruct((bsz, s, 512), BF16),
        scratch_shapes=[pltpu.VMEM((2, 4 * HEAD_DIM, tq), F8),
                        pltpu.VMEM((2, 1, tq), F32),
                        pltpu.VMEM((2, 1, tq), F32),
                        pltpu.VMEM((2, dve, tq), F32),
                        pltpu.VMEM((1, LANES), F32),
                        pltpu.VMEM((1, LANES), F32),
                        pltpu.VMEM((2 if mode == "diff" else 1, s, 4 * HEAD_DIM), F8),
                        pltpu.VMEM((2, ck, tq), BF16),
                        pltpu.VMEM((2, ck, tq), BF16)],
        compiler_params=_params("parallel", "parallel", "arbitrary"),
        name="flash_" + mode,
    )(qt, k, vt, *extra)


def _dilated_kernel(q_ref, kp_ref, km_ref, kn_ref, vp_ref, vm_ref, vn_ref,
                    o_ref, m_ref, l_ref, *, tq, sub_len):
    i0 = pl.program_id(2) * tq
    q = q_ref[...]
    kw = jnp.concatenate([kp_ref[...], km_ref[...], kn_ref[...]], axis=0)
    vw = jnp.concatenate([vp_ref[...], vm_ref[...], vn_ref[...]], axis=0)
    nk = tq + 2 * D_HALF_WINDOW
    t = lax.broadcasted_iota(jnp.int32, (tq, nk), 0)
    w = lax.broadcasted_iota(jnp.int32, (tq, nk), 1)
    off = w - D_HALF_WINDOW - t
    j = i0 - D_HALF_WINDOW + w
    valid = (off >= -D_HALF_WINDOW) & (off <= D_HALF_WINDOW) & (j >= 0) & (j < sub_len)
    lane = lax.broadcasted_iota(jnp.int32, (1, D_WIDTH), 1)
    o_acc = jnp.zeros((tq, D_WIDTH), F32)
    m_acc = jnp.zeros((tq, D_WIDTH), F32)
    l_acc = jnp.zeros((tq, D_WIDTH), F32)
    for h in range(D_SLOTS):
        in_head = (lane >= h * HEAD_DIM) & (lane < (h + 1) * HEAD_DIM)
        qh = jnp.where(in_head, q, jnp.zeros_like(q))
        s = lax.dot_general(qh, kw, (((1,), (1,)), ((), ())), preferred_element_type=F32)
        s = jnp.where(valid, s, NEG_BIG)
        mh = jnp.max(s, axis=1, keepdims=True)
        p = jnp.exp2(s - mh)
        lh = jnp.sum(p, axis=1, keepdims=True)
        oh = _dot(p.astype(BF16), vw)
        sel = in_head.astype(F32)
        o_acc = o_acc + oh * sel
        m_acc = m_acc + mh * sel
        l_acc = l_acc + lh * sel
    o_ref[...] = o_acc
    m_ref[...] = m_acc
    l_ref[...] = l_acc


def _dilated(qd, kd, vd, pat):
    bsz, dilation, sub_len, _ = qd.shape
    tq = min(256, sub_len)
    n_halo = sub_len // D_HALF_WINDOW
    per_tile = tq // D_HALF_WINDOW
    main = pl.BlockSpec((None, None, tq, D_WIDTH), lambda b, c, i: (b, c, i, 0))
    prev = pl.BlockSpec((None, None, D_HALF_WINDOW, D_WIDTH),
                        lambda b, c, i: (b, c, jnp.maximum(i * per_tile - 1, 0), 0))
    nxt = pl.BlockSpec((None, None, D_HALF_WINDOW, D_WIDTH),
                       lambda b, c, i: (b, c, jnp.minimum((i + 1) * per_tile, n_halo - 1), 0))
    out = jax.ShapeDtypeStruct((bsz, dilation, sub_len, D_WIDTH), F32)
    return pl.pallas_call(
        functools.partial(_dilated_kernel, tq=tq, sub_len=sub_len),
        grid=(bsz, dilation, sub_len // tq),
        in_specs=[main, prev, main, nxt, prev, main, nxt],
        out_specs=[main] * 3,
        out_shape=[out] * 3,
        compiler_params=_params("parallel", "parallel", "parallel"),
        name="dilated_p%d" % pat,
    )(qd, kd, kd, kd, vd, vd, vd)


def _mix0_kernel(x_ref, u_ref, vn_ref, yb_ref, ws_ref, bs_ref, w_ref, g_ref, b_ref, o_ref, *, tm):
    lane = lax.broadcasted_iota(jnp.int32, (A_CHUNK, LANES), 1)
    left = lane < HEAD_DIM
    rows = []
    for n in range(tm // A_CHUNK):
        r = slice(n * A_CHUNK, (n + 1) * A_CHUNK)
        cols = []
        for j in range(A_WIDTH // LANES):
            v = vn_ref[r, j * LANES:(j + 1) * LANES]
            zero = jnp.zeros_like(v)
            stacked = jnp.concatenate([jnp.where(left, v, zero), jnp.where(left, zero, v)], axis=0)
            cols.append(_dot(ws_ref[j], stacked))
        rows.append(jnp.concatenate(cols, axis=1) + bs_ref[...])
    mixed = jnp.concatenate(rows, axis=0)
    ya = (u_ref[...] * mixed).astype(BF16)
    y = _dot(ya, w_ref[0:A_WIDTH, :]) + _dot(yb_ref[...], w_ref[A_WIDTH:, :])
    o_ref[...] = _layer_norm(ALPHA * x_ref[...] + y, g_ref[...], b_ref[...])


def _mix0(x, u, vn, yb, ws_pair, bs_x, w_out, g, b, tm):
    bsz, s, d = x.shape
    tok = lambda width: pl.BlockSpec((None, tm, width), lambda bb, i: (bb, i, 0))
    return pl.pallas_call(
        functools.partial(_mix0_kernel, tm=tm),
        grid=(bsz, s // tm),
        in_specs=[tok(d), tok(512), tok(512), tok(512), _resident(ws_pair.shape), _resident(bs_x.shape),
                  _resident(w_out.shape), _resident((1, d)), _resident((1, d))],
        out_specs=tok(d),
        out_shape=jax.ShapeDtypeStruct((bsz, s, d), F32),
        compiler_params=_params("parallel", "parallel"),
        name="mix0",
    )(x, u, vn, yb, ws_pair, bs_x, w_out, g, b)


def _mix1_kernel(x_ref, yc_ref, *refs, tm):
    pat_refs, (w_ref, g_ref, b_ref, o_ref), scr_refs = refs[:9], refs[9:13], refs[13:]

    def token_major(idx):
        dil = D_PATTERNS[idx // 3][1]
        ref = pat_refs[idx]
        if dil == 1:
            return ref[0]
        scr = scr_refs[idx - 3]
        for c in range(dil):
            for j in range(D_WIDTH // LANES):
                scr[j, pl.ds(c, tm // dil, stride=dil), :] = ref[c, :, j * LANES:(j + 1) * LANES]
        return jnp.concatenate([scr[j] for j in range(D_WIDTH // LANES)], axis=1)

    vals = [token_major(i) for i in range(9)]
    os_, ms_, ls_ = vals[0::3], vals[1::3], vals[2::3]
    m_all = jnp.maximum(jnp.maximum(ms_[0], ms_[1]), ms_[2])
    num = jnp.zeros(m_all.shape, F32)
    den = jnp.zeros(m_all.shape, F32)
    for p in range(3):
        wgt = jnp.exp2(ms_[p] - m_all)
        num = num + wgt * os_[p]
        den = den + wgt * ls_[p]
    yd = (num * (1.0 / den)).astype(BF16)
    y = _dot(yc_ref[...], w_ref[0:512, :]) + _dot(yd, w_ref[512:, :])
    o_ref[...] = _layer_norm(ALPHA * x_ref[...] + y, g_ref[...], b_ref[...])


def _mix1(x, yc, pats, w_out, g, b, tm):
    bsz, s, d = x.shape
    tok = lambda width: pl.BlockSpec((None, tm, width), lambda bb, i: (bb, i, 0))
    flat = [a for oml in pats for a in oml]
    pat_specs = [pl.BlockSpec((None, a.shape[1], tm // a.shape[1], D_WIDTH), lambda bb, i: (bb, 0, i, 0))
                 for a in flat]
    return pl.pallas_call(
        functools.partial(_mix1_kernel, tm=tm),
        grid=(bsz, s // tm),
        in_specs=[tok(d), tok(512)] + pat_specs
                 + [_resident(w_out.shape), _resident((1, d)), _resident((1, d))],
        out_specs=tok(d),
        out_shape=jax.ShapeDtypeStruct((bsz, s, d), F32),
        scratch_shapes=[pltpu.VMEM((D_WIDTH // LANES, tm, LANES), F32)] * 6,
        compiler_params=_params("parallel", "parallel"),
        name="mix1",
    )(x, yc, *flat, w_out, g, b)


def _ffn_kernel(xp_ref, xm_ref, xn_ref, wa_ref, wg_ref, ca_ref, cg_ref, wd_ref, g_ref, b_ref,
                o_ref, xe_ref, *, tm, tiles_per_seq):
    i = pl.program_id(0) % tiles_per_seq
    halo = 8
    prev = jnp.where(i > 0, xp_ref[...], 0.0)
    nxt = jnp.where(i < tiles_per_seq - 1, xn_ref[...], 0.0)
    xe_ref[...] = jnp.concatenate([prev, xm_ref[...], nxt], axis=0).astype(BF16)
    ext = tm + 2 * halo

    def conv(h, cw):
        mid = slice(halo, halo + tm)
        return (cw[0:1] * pltpu.roll(h, 1, 0)[mid] + cw[1:2] * h[mid]
                + cw[2:3] * pltpu.roll(h, ext - 1, 0)[mid] + cw[3:4])

    n_c = wa_ref.shape[0]
    y = None
    acts = []
    for c in range(n_c):
        xe = xe_ref[...]
        a = conv(_dot(xe, wa_ref[c]), ca_ref[c])
        gt = conv(_dot(xe, wg_ref[c]), cg_ref[c])
        acts.append((jax.nn.gelu(gt) * a).astype(BF16))
        if len(acts) == FF_DOWN_GROUP or c == n_c - 1:
            c0 = c + 1 - len(acts)
            part = _dot(jnp.concatenate(acts, axis=1), wd_ref[c0 * FF_CHUNK:(c + 1) * FF_CHUNK, :])
            y = part if y is None else y + part
            acts = []
    o_ref[...] = _layer_norm(ALPHA * xm_ref[...] + y, g_ref[...], b_ref[...])


def _ffn(x, wa, wg, ca, cg, wd, g, b, tm):
    bsz, s, d = x.shape
    n_tok = bsz * s
    xf = x.reshape(n_tok, d)
    tiles_per_seq = s // tm
    r8 = tm // 8
    n8 = n_tok // 8
    out = pl.pallas_call(
        functools.partial(_ffn_kernel, tm=tm, tiles_per_seq=tiles_per_seq),
        grid=(n_tok // tm,),
        in_specs=[pl.BlockSpec((8, d), lambda i: (jnp.maximum(i * r8 - 1, 0), 0)),
                  pl.BlockSpec((tm, d), lambda i: (i, 0)),
                  pl.BlockSpec((8, d), lambda i: (jnp.minimum((i + 1) * r8, n8 - 1), 0)),
                  _resident(wa.shape), _resident(wg.shape), _resident(ca.shape), _resident(cg.shape),
                  _resident(wd.shape), _resident((1, d)), _resident((1, d))],
        out_specs=pl.BlockSpec((tm, d), lambda i: (i, 0)),
        out_shape=jax.ShapeDtypeStruct((n_tok, d), F32),
        scratch_shapes=[pltpu.VMEM((tm + 16, d), BF16)],
        compiler_params=_params("parallel"),
        name="conv_ffn",
    )(xf, xf, xf, wa, wg, ca, cg, wd, g, b)
    return out.reshape(bsz, s, d)


def _rope_tables(pos, n_dims, theta):
    inv = jnp.power(theta, -jnp.arange(0, n_dims, 2, dtype=F32) / n_dims)
    ang = pos.astype(F32)[:, None] * inv[None, :]
    return jnp.cos(ang), jnp.sin(ang)


def _lane_tables(parts, s):
    cs, lo, hi = [], [], []
    zero = None
    for cos, sin in parts:
        zero = jnp.zeros_like(sin)
        cs += [cos, cos]
        lo += [-sin, zero]
        hi += [zero, sin]
    used = sum(a.shape[1] for a in cs)
    pad1 = jnp.ones((s, HEAD_DIM - used), F32)
    pad0 = jnp.zeros((s, HEAD_DIM - used), F32)
    build = lambda xs, pad: jnp.tile(jnp.concatenate(xs + [pad], axis=1), (1, LANES // HEAD_DIM))
    return build(cs, pad1), build(lo, pad0), build(hi, pad0)


def _ffn_weights(w_up, conv_w, conv_b, w_down):
    d = w_up.shape[0]
    n_c = D_FF // FF_CHUNK
    chunks = lambda w: w.reshape(d, n_c, FF_CHUNK).transpose(1, 0, 2).astype(BF16)
    wa, wg = chunks(w_up[:, :D_FF]), chunks(w_up[:, D_FF:])

    def conv_pack(cw, cb):
        rows = jnp.concatenate([cw, cb[None, :], jnp.zeros((4, D_FF), F32)], axis=0)
        return rows.reshape(8, n_c, FF_CHUNK).transpose(1, 0, 2)

    ca = conv_pack(conv_w[:, :D_FF], conv_b[:D_FF])
    cg = conv_pack(conv_w[:, D_FF:], conv_b[D_FF:])
    wd = w_down.astype(BF16)
    return wa, wg, ca, cg, wd


def _trunk(x, p, tm=512, tq=1024):
    bsz, s, d = x.shape
    tm, tq, ck = min(tm, s), min(tq, s), min(KV_CHUNK, s)
    pos = jnp.arange(s)
    tabs_p = _lane_tables([_rope_tables(pos, ROPE_DIMS, ROPE_THETA)], s)
    tabs_a = _lane_tables([_rope_tables(pos // GRID_W, HEAD_DIM // 2, AXIAL_THETA),
                           _rope_tables(pos % GRID_W, HEAD_DIM // 2, AXIAL_THETA)], s)
    row = lambda v: v.reshape(1, -1)

    u, vn, q0t, k0, v0t = _inproj0(x, p["w_in0"], row(p["a_ln_g"]), row(p["a_ln_b"]), tabs_p, tm, ck)
    yb = _flash("diff", q0t, k0, v0t,
                [row(p["b_lam_q1"]), row(p["b_lam_k1"]), row(p["b_lam_q2"]), row(p["b_lam_k2"]),
                 p["b_subln_g"].reshape(-1, 1)], tq)
    x = _mix0(x, u, vn, yb, p["ws_pair"], p["bs_x"], p["w_out0"], row(p["ln0a_g"]), row(p["ln0a_b"]), tm)
    x = _ffn(x, *p["ffn0"], row(p["ln0b_g"]), row(p["ln0b_b"]), tm)

    qct, kc, vct, *qkv_d = _inproj1(x, p["w_in1"], p["qg"], p["kg"], p["ones_blk"], tabs_p, tabs_a, tm, ck)
    yc = _flash("gqa", qct, kc, vct, [], tq)
    pats = [_dilated(*qkv_d[3 * pi:3 * pi + 3], pi) for pi in range(len(D_PATTERNS))]
    x = _mix1(x, yc, pats, p["w_out1"], row(p["ln1a_g"]), row(p["ln1a_b"]), tm)
    x = _ffn(x, *p["ffn1"], row(p["ln1b_g"]), row(p["ln1b_b"]), tm)
    return x


def _prepare(w_in0, a_ln_g, a_ln_b, a_ws, a_bs, b_lam_q1, b_lam_k1, b_lam_q2, b_lam_k2, b_subln_g,
             w_out0, ln0a_g, ln0a_b, w_up0, conv_w0, conv_b0, w_down0, ln0b_g, ln0b_b,
             w_in1, c_qnorm_g, c_knorm_g, w_out1, ln1a_g, ln1a_b, w_up1, conv_w1, conv_b1, w_down1,
             ln1b_g, ln1b_b):
    grp = jnp.arange(LANES) // HEAD_DIM
    return dict(
        w_in0=w_in0.astype(BF16), a_ln_g=a_ln_g, a_ln_b=a_ln_b,
        ws_pair=a_ws.reshape(A_GROUPS // 2, 2, A_CHUNK, A_CHUNK).transpose(0, 2, 1, 3)
        .reshape(A_GROUPS // 2, A_CHUNK, 2 * A_CHUNK).astype(BF16),
        bs_x=jnp.repeat(a_bs.T, HEAD_DIM, axis=1),
        b_lam_q1=b_lam_q1, b_lam_k1=b_lam_k1, b_lam_q2=b_lam_q2, b_lam_k2=b_lam_k2, b_subln_g=b_subln_g,
        w_out0=w_out0.astype(BF16), ln0a_g=ln0a_g, ln0a_b=ln0a_b,
        ffn0=_ffn_weights(w_up0, conv_w0, conv_b0, w_down0), ln0b_g=ln0b_g, ln0b_b=ln0b_b,
        w_in1=w_in1.astype(BF16),
        qg=jnp.tile(c_qnorm_g, LANES // HEAD_DIM).reshape(1, LANES),
        kg=jnp.tile(c_knorm_g, LANES // HEAD_DIM).reshape(1, LANES),
        ones_blk=(grp[:, None] == grp[None, :]).astype(BF16),
        w_out1=w_out1.astype(BF16), ln1a_g=ln1a_g, ln1a_b=ln1a_b,
        ffn1=_ffn_weights(w_up1, conv_w1, conv_b1, w_down1), ln1b_g=ln1b_g, ln1b_b=ln1b_b,
    )


def kernel(x_prompt, x_sample, w_in0, a_ln_g, a_ln_b, a_ws, a_bs, b_lam_q1, b_lam_k1, b_lam_q2, b_lam_k2, b_subln_g, w_out0, ln0a_g, ln0a_b, w_up0, conv_w0, conv_b0, w_down0, ln0b_g, ln0b_b, w_in1, c_qnorm_g, c_knorm_g, w_out1, ln1a_g, ln1a_b, w_up1, conv_w1, conv_b1, w_down1, ln1b_g, ln1b_b):
    p = _prepare(w_in0, a_ln_g, a_ln_b, a_ws, a_bs, b_lam_q1, b_lam_k1, b_lam_q2, b_lam_k2, b_subln_g,
                 w_out0, ln0a_g, ln0a_b, w_up0, conv_w0, conv_b0, w_down0, ln0b_g, ln0b_b,
                 w_in1, c_qnorm_g, c_knorm_g, w_out1, ln1a_g, ln1a_b, w_up1, conv_w1, conv_b1, w_down1,
                 ln1b_g, ln1b_b)
    return (_trunk(x_prompt, p), _trunk(x_sample, p))
```
